```python
import math
import jax
import jax.numpy as jnp
from jax import lax
import numpy as np

D_MODEL = 1024
BATCH = 8
SEQ = 2048
DEPTH = 4

CTX_LEN = 256
GRID_W = 64
DA_HEADS = 4
DA_QK_DIM = 64
DA_V_DIM = 2 * DA_QK_DIM
DA_WIDTH = DA_HEADS * DA_V_DIM
HG_HEADS = 4
HG_K_DIM = 128
HG_V_DIM = (D_MODEL - DA_WIDTH) // HG_HEADS
HG_KW = HG_HEADS * HG_K_DIM
HG_VW = HG_HEADS * HG_V_DIM
HG_CHUNK = 64
Q_BLOCK = 128
ROPE_BASE = 10000.0
CONV_WIDTH = 31
D_FF = 2816
N_MOD = 9
N_EVEN = (DEPTH + 1) // 2
N_ODD = DEPTH // 2
LAST_CTX_READ = 2 * ((DEPTH - 1) // 2)
DEEPNORM_ALPHA = (2 * DEPTH) ** 0.25
DEEPNORM_BETA = (8 * DEPTH) ** -0.25
EPS = 1e-5
EVEN_IN_WIDTHS = (2 * DA_HEADS * DA_QK_DIM, 2 * DA_HEADS * DA_QK_DIM, DA_WIDTH,
                  HG_KW, HG_KW, HG_KW, HG_VW, HG_VW)
EVEN_IN = sum(EVEN_IN_WIDTHS)
EVEN_SPLITS = tuple(np.cumsum(EVEN_IN_WIDTHS)[:-1].tolist())

kernel_name = "hybrid_diffattn_hgrn2_conformer_dit"


def layer_norm(x, g, b):
    xf = x.astype(jnp.float32)
    mu = jnp.mean(xf, axis=-1, keepdims=True)
    var = jnp.mean(jnp.square(xf - mu), axis=-1, keepdims=True)
    return ((xf - mu) * lax.rsqrt(var + EPS) * g + b).astype(x.dtype)


def rms_norm(x, g):
    xf = x.astype(jnp.float32)
    return (xf * lax.rsqrt(jnp.mean(xf * xf, axis=-1, keepdims=True) + EPS) * g).astype(x.dtype)


def modulate(h, shift, scale):
    return h * (1.0 + scale) + shift


def post_norm_residual(x, y, gate, g, b):
    return layer_norm(DEEPNORM_ALPHA * x + gate * y, g, b)


def swiglu(h, w13, w2):
    a, b = jnp.split(h @ w13, 2, axis=-1)
    return (jax.nn.silu(a) * b) @ w2


def macaron_ffn_step(x, mods, sub, g, b, w13, w2):
    fi = sub // 2
    y = swiglu(modulate(x, mods[3 * sub], mods[3 * sub + 1]), w13[fi], w2[fi])
    return post_norm_residual(x, 0.5 * y, mods[3 * sub + 2], g[sub], b[sub])


def rotate_half(x):
    x1, x2 = jnp.split(x, 2, axis=-1)
    return jnp.concatenate([-x2, x1], axis=-1)


def axial_rope_tables(n_tokens):
    rows = n_tokens // GRID_W
    r, col = jnp.meshgrid(jnp.arange(rows), jnp.arange(GRID_W), indexing="ij")
    half = DA_QK_DIM // 2
    inv_freq = ROPE_BASE ** (-jnp.arange(0, half, 2, dtype=jnp.float32) / half)

    def table(pos):
        ang = pos.reshape(-1).astype(jnp.float32)[:, None] * inv_freq[None, :]
        ang = jnp.concatenate([ang, ang], axis=-1)
        return jnp.cos(ang), jnp.sin(ang)

    return table(r), table(col)


def apply_axial_rope(x, tables):
    (cos_r, sin_r), (cos_c, sin_c) = tables
    half = DA_QK_DIM // 2
    bc = lambda t: t[None, :, None, None, :]
    xr, xc = x[..., :half], x[..., half:]
    xr = xr * bc(cos_r) + rotate_half(xr) * bc(sin_r)
    xc = xc * bc(cos_c) + rotate_half(xc) * bc(sin_c)
    return jnp.concatenate([xr, xc], axis=-1).astype(x.dtype)


def da_heads(qa, ka, va):
    sh = qa.shape[:2]
    q = qa.reshape(*sh, DA_HEADS, 2, DA_QK_DIM) * (DA_QK_DIM ** -0.5)
    k = ka.reshape(*sh, DA_HEADS, 2, DA_QK_DIM)
    v = va.reshape(*sh, DA_HEADS, DA_V_DIM)
    return q, k, v


def diff_softmax_mix(q, k, v, lam):
    s = jnp.einsum("bqhcd,bkhcd->bhcqk", q, k).astype(jnp.float32)
    p = jax.nn.softmax(s, axis=-1)
    a = p[:, :, 0] - lam * p[:, :, 1]
    return jnp.einsum("bhqk,bkhd->bqhd", a.astype(v.dtype), v)


def diff_attention_blocks(q, k_all, v_all, lam):
    b, t = q.shape[:2]
    nb = t // Q_BLOCK
    qb = jnp.swapaxes(q.reshape(b, nb, Q_BLOCK, DA_HEADS, 2, DA_QK_DIM), 0, 1)
    ob = lax.map(lambda qq: diff_softmax_mix(qq, k_all, v_all, lam), qb)
    return jnp.swapaxes(ob, 0, 1).reshape(b, t, DA_HEADS, DA_V_DIM)


def hgrn_inputs(hq, hf_f, hf_b, hi, lb_f, lb_b):
    sh = hq.shape[:2]
    heads = lambda a, d: a.reshape(*sh, HG_HEADS, d)
    q = jax.nn.silu(heads(hq, HG_K_DIM))
    f_f = lb_f + (1.0 - lb_f) * jax.nn.sigmoid(heads(hf_f, HG_K_DIM).astype(jnp.float32))
    f_b = lb_b + (1.0 - lb_b) * jax.nn.sigmoid(heads(hf_b, HG_K_DIM).astype(jnp.float32))
    v = heads(hi, HG_V_DIM)
    return q, 1.0 - f_f, jnp.log(f_f), 1.0 - f_b, jnp.log(f_b), v


def hgrn2_chunk_scan(q, k, v, log_f, s0, with_output):
    b, t, h, _ = q.shape
    n = t // HG_CHUNK

    def chunks(a):
        return a.astype(jnp.float32).reshape(b, n, HG_CHUNK, h, a.shape[-1]).transpose(1, 0, 3, 2, 4)

    causal = jnp.tril(jnp.ones((HG_CHUNK, HG_CHUNK), dtype=bool))

    def step(s, xs):
        qc, kc, vc, gc = xs
        g = jnp.cumsum(gc, axis=-2)
        g_last = g[:, :, -1, :]
        s_new = jnp.exp(g_last)[..., None] * s + jnp.einsum(
            "bhsk,bhsv->bhkv", kc * jnp.exp(g_last[:, :, None, :] - g), vc)
        if not with_output:
            return s_new, None
        o_inter = jnp.einsum("bhtk,bhkv->bhtv", qc * jnp.exp(g), s)
        rel = jnp.where(causal[:, :, None], g[:, :, :, None, :] - g[:, :, None, :, :], -jnp.inf)
        att = jnp.einsum("bhtk,bhsk,bhtsk->bhts", qc, kc, jnp.exp(rel))
        return s_new, o_inter + jnp.einsum("bhts,bhsv->bhtv", att, vc)

    s_fin, o = lax.scan(step, s0, (chunks(q), chunks(k), chunks(v), chunks(log_f)))
    if with_output:
        o = o.transpose(1, 0, 3, 2, 4).reshape(b, t, h, -1)
    return o, s_fin


def hgrn2_bidir(q, k_f, lf_f, k_b, lf_b, v, s0_f, s0_b, with_output):
    flip = lambda a: jnp.flip(a, axis=1)
    o_f, s_f = hgrn2_chunk_scan(q, k_f, v, lf_f, s0_f, with_output)
    o_b, s_b = hgrn2_chunk_scan(flip(q), flip(k_b), flip(v), flip(lf_b), s0_b, with_output)
    o = o_f + flip(o_b) if with_output else None
    return o, s_f, s_b


def even_merge(a, o, g, da_norm, hg_norm, lam_init, w_out):
    sh = a.shape[:2]
    a = rms_norm(a, da_norm.reshape(DA_HEADS, DA_V_DIM)) * (1.0 - lam_init)
    o = rms_norm(o.astype(g.dtype), hg_norm.reshape(HG_HEADS, HG_V_DIM))
    y = jnp.concatenate([a.reshape(*sh, DA_WIDTH), o.reshape(*sh, HG_VW) * jax.nn.silu(g)], axis=-1)
    return y @ w_out


def even_mixer(hl, hc, layer, w_in, w_out, lam_vecs, da_norm, lower_bound, hg_norm, rope, ctx_out):
    lam_init = 0.8 - 0.6 * math.exp(-0.3 * layer)
    lv = lam_vecs.astype(jnp.float32)
    lam = jnp.exp(jnp.sum(lv[0] * lv[1])) - jnp.exp(jnp.sum(lv[2] * lv[3])) + lam_init
    lb_f, lb_b = [t.reshape(HG_HEADS, HG_K_DIM) for t in jnp.split(lower_bound, 2)]
    qa_l, ka_l, va_l, hq_l, hff_l, hfb_l, hi_l, hg_l = jnp.split(hl @ w_in, EVEN_SPLITS, axis=-1)
    qa_c, ka_c, va_c, hq_c, hff_c, hfb_c, hi_c, hg_c = jnp.split(hc @ w_in, EVEN_SPLITS, axis=-1)
    q_l, k_l, v_l = da_heads(qa_l, ka_l, va_l)
    q_c, k_c, v_c = da_heads(qa_c, ka_c, va_c)
    q_l = apply_axial_rope(q_l, rope)
    k_l = apply_axial_rope(k_l, rope)
    k_all = jnp.concatenate([k_l, k_c], axis=1)
    v_all = jnp.concatenate([v_l, v_c], axis=1)
    a_l = diff_attention_blocks(q_l, k_all, v_all, lam)
    z = jnp.zeros((hc.shape[0], HG_HEADS, HG_K_DIM, HG_V_DIM), jnp.float32)
    o_c, s_f, s_b = hgrn2_bidir(*hgrn_inputs(hq_c, hff_c, hfb_c, hi_c, lb_f, lb_b), z, z, ctx_out)
    o_l, _, _ = hgrn2_bidir(*hgrn_inputs(hq_l, hff_l, hfb_l, hi_l, lb_f, lb_b), s_f, s_b, True)
    y_l = even_merge(a_l, o_l, hg_l, da_norm, hg_norm, lam_init, w_out)
    y_c = None
    if ctx_out:
        a_c = diff_softmax_mix(q_c, k_c, v_c, lam)
        y_c = even_merge(a_c, o_c, hg_c, da_norm, hg_norm, lam_init, w_out)
    return y_l, y_c


def conformer_conv(h, w_pw1, b_pw1, w_dw, b_dw, ln_g, ln_b, w_pw2, b_pw2):
    a, gate = jnp.split(h @ w_pw1 + b_pw1, 2, axis=-1)
    u = a * jax.nn.sigmoid(gate)
    u = lax.conv_general_dilated(
        u, w_dw[:, None, :].astype(u.dtype), window_strides=(1,),
        padding=[(CONV_WIDTH // 2, CONV_WIDTH // 2)],
        dimension_numbers=("NWC", "WIO", "NWC"), feature_group_count=u.shape[-1]) + b_dw
    u = jax.nn.silu(layer_norm(u, ln_g, ln_b))
    return u @ w_pw2 + b_pw2


def setup_inputs(seed: int = 0) -> dict:
    key = jax.random.key(seed)
    ks = iter(jax.random.split(key, 24))
    nrm = lambda shape, scale: jax.random.normal(next(ks), shape, jnp.float32) * scale
    D = D_MODEL
    return {
        "x": nrm((BATCH, SEQ, D), 1.0),
        "c": nrm((BATCH, D), 1.0),
        "ctx": nrm((BATCH, CTX_LEN, D), 1.0),
        "c_ctx": nrm((D,), 1.0),
        "w_mod": nrm((DEPTH, D, N_MOD * D), D ** -0.5),
        "b_mod": nrm((DEPTH, N_MOD * D), 0.01),
        "ln_g": 1.0 + nrm((DEPTH, 3, D), 0.02),
        "ln_b": nrm((DEPTH, 3, D), 0.01),
        "ffn_w13": nrm((DEPTH, 2, D, 2 * D_FF), D ** -0.5),
        "ffn_w2": nrm((DEPTH, 2, D_FF, D), DEEPNORM_BETA * D_FF ** -0.5),
        "ev_w_in": nrm((N_EVEN, D, EVEN_IN), D ** -0.5),
        "ev_w_out": nrm((N_EVEN, DA_WIDTH + HG_VW, D), DEEPNORM_BETA * (DA_WIDTH + HG_VW) ** -0.5),
        "da_lambda": nrm((N_EVEN, 4, DA_QK_DIM), 0.1),
        "da_norm": 1.0 + nrm((N_EVEN, DA_WIDTH), 0.02),
        "hg_lb": nrm((N_EVEN, 2 * HG_KW), 0.5),
        "hg_norm": 1.0 + nrm((N_EVEN, HG_VW), 0.02),
        "cv_w_pw1": nrm((N_ODD, D, 2 * D), D ** -0.5),
        "cv_b_pw1": nrm((N_ODD, 2 * D), 0.01),
        "cv_w_dw": nrm((N_ODD, CONV_WIDTH, D), CONV_WIDTH ** -0.5),
        "cv_b_dw": nrm((N_ODD, D), 0.01),
        "cv_ln_g": 1.0 + nrm((N_ODD, D), 0.02),
        "cv_ln_b": nrm((N_ODD, D), 0.01),
        "cv_w_pw2": nrm((N_ODD, D, D), DEEPNORM_BETA * D ** -0.5),
        "cv_b_pw2": nrm((N_ODD, D), 0.01),
    }


def reference(x, c, ctx, c_ctx, w_mod, b_mod, ln_g, ln_b, ffn_w13, ffn_w2,
              ev_w_in, ev_w_out, da_lambda, da_norm, hg_lb, hg_norm,
              cv_w_pw1, cv_b_pw1, cv_w_dw, cv_b_dw, cv_ln_g, cv_ln_b, cv_w_pw2, cv_b_pw2):
    rope = axial_rope_tables(x.shape[1])
    lb_w = jax.nn.softmax(hg_lb.astype(jnp.float32), axis=0)
    lower_bounds = jnp.cumsum(lb_w, axis=0) - lb_w[:1]
    xl, xc = x, ctx
    hc = None
    for layer in range(DEPTH):
        j = layer // 2
        even = layer % 2 == 0
        ctx_pre = layer <= LAST_CTX_READ
        ctx_full = layer < LAST_CTX_READ
        mods_l = jnp.split((jax.nn.silu(c) @ w_mod[layer] + b_mod[layer])[:, None, :], N_MOD, axis=-1)
        mods_c = jnp.split((jax.nn.silu(c_ctx) @ w_mod[layer] + b_mod[layer])[None, None, :], N_MOD, axis=-1)
        p_ffn = (ln_g[layer], ln_b[layer], ffn_w13[layer], ffn_w2[layer])
        xl = macaron_ffn_step(xl, mods_l, 0, *p_ffn)
        hl = modulate(xl, mods_l[3], mods_l[4])
        if ctx_pre:
            xc = macaron_ffn_step(xc, mods_c, 0, *p_ffn)
            hc = modulate(xc, mods_c[3], mods_c[4])
        if even:
            yl, yc = even_mixer(hl, hc, layer, ev_w_in[j], ev_w_out[j], da_lambda[j], da_norm[j],
                                lower_bounds[j], hg_norm[j], rope, ctx_full)
        else:
            p_cv = (cv_w_pw1[j], cv_b_pw1[j], cv_w_dw[j], cv_b_dw[j], cv_ln_g[j], cv_ln_b[j],
                    cv_w_pw2[j], cv_b_pw2[j])
            yl = conformer_conv(hl, *p_cv)
            yc = conformer_conv(hc, *p_cv) if ctx_full else None
        xl = post_norm_residual(xl, yl, mods_l[5], ln_g[layer, 1], ln_b[layer, 1])
        xl = macaron_ffn_step(xl, mods_l, 2, *p_ffn)
        if ctx_full:
            xc = post_norm_residual(xc, yc, mods_c[5], ln_g[layer, 1], ln_b[layer, 1])
            xc = macaron_ffn_step(xc, mods_c, 2, *p_ffn)
    return xl
```

```python
import functools
import math

import jax
import jax.numpy as jnp
from jax import lax
from jax.experimental import pallas as pl
from jax.experimental.pallas import tpu as pltpu

F32 = jnp.float32
BF16 = jnp.bfloat16

D_MODEL = 1024
DEPTH = 4
GRID_W = 64
N_HEADS = 4
HEAD_W = 128
QK_DIM = 64
GROUP_W = N_HEADS * HEAD_W
D_FF = 2816
N_MOD = 9
CONV_WIDTH = 31
CONV_HALO = 16
ROPE_BASE = 10000.0
LAST_CTX_READ = 2 * ((DEPTH - 1) // 2)
ALPHA = (2 * DEPTH) ** 0.25
EPS = 1e-5
HG_CHUNK = 128
EVEN_IN = 8 * GROUP_W
VMEM_LIMIT = 56 * 1024 * 1024

NT_DIMS = (((1,), (1,)), ((), ()))
TN_DIMS = (((0,), (0,)), ((), ()))


def _params(*sem):
    return pltpu.CompilerParams(dimension_semantics=sem, vmem_limit_bytes=VMEM_LIMIT)


def _layer_norm(z, g, b):
    mu = jnp.mean(z, axis=-1, keepdims=True)
    zc = z - mu
    var = jnp.mean(zc * zc, axis=-1, keepdims=True)
    return zc * lax.rsqrt(var + EPS) * g + b


def _rms_norm(z, g):
    return z * lax.rsqrt(jnp.mean(z * z, axis=-1, keepdims=True) + EPS) * g


def _silu(a):
    return a * jax.nn.sigmoid(a)


def _row_tile(t):
    return min(t, 512)


def _mods_kernel(c_ref, w_ref, b_ref, o_ref):
    a = _silu(c_ref[...]).astype(BF16)
    o_ref[...] = jnp.dot(a, w_ref[...].astype(BF16), preferred_element_type=F32) + b_ref[...]


def _mods(cc, w_mod, b_mod):
    n = cc.shape[0]
    tn = 1024
    return pl.pallas_call(
        _mods_kernel,
        out_shape=jax.ShapeDtypeStruct((DEPTH, n, N_MOD * D_MODEL), F32),
        grid=(DEPTH, N_MOD * D_MODEL // tn),
        in_specs=[
            pl.BlockSpec((n, D_MODEL), lambda l, j: (0, 0)),
            pl.BlockSpec((None, D_MODEL, tn), lambda l, j: (l, 0, j)),
            pl.BlockSpec((None, 1, tn), lambda l, j: (l, 0, j)),
        ],
        out_specs=pl.BlockSpec((None, n, tn), lambda l, j: (l, 0, j)),
        compiler_params=_params("parallel", "parallel"),
        name="mods",
    )(cc, w_mod, b_mod.reshape(DEPTH, 1, N_MOD * D_MODEL))


def _ffn_kernel(x_ref, mod_ref, w1_ref, w3_ref, w2_ref, g_ref, b_ref, o_ref, h_ref, acc_ref, *, sub):
    k = pl.program_id(2)

    @pl.when(k == 0)
    def _():
        shift = mod_ref[3 * sub:3 * sub + 1, :]
        scale = mod_ref[3 * sub + 1:3 * sub + 2, :]
        h_ref[...] = (x_ref[...] * (1.0 + scale) + shift).astype(BF16)
        acc_ref[...] = jnp.zeros_like(acc_ref)

    h = h_ref[...]
    a = jnp.dot(h, w1_ref[...], preferred_element_type=F32)
    b = jnp.dot(h, w3_ref[...], preferred_element_type=F32)
    acc_ref[...] += jnp.dot((_silu(a) * b).astype(BF16), w2_ref[...], preferred_element_type=F32)

    @pl.when(k == pl.num_programs(2) - 1)
    def _():
        gate = mod_ref[3 * sub + 2:3 * sub + 3, :]
        z = ALPHA * x_ref[...] + gate * (0.5 * acc_ref[...])
        o_ref[...] = _layer_norm(z, g_ref[...], b_ref[...])


def _ffn(x, mods, w13, w2, g, b, sub):
    bsz, t, d = x.shape
    tm = _row_tile(t)
    tf = D_FF // 2
    nk = D_FF // tf
    per_batch = mods.shape[0] > 1
    mod_map = (lambda bi, i, k: (bi, 0, 0)) if per_batch else (lambda bi, i, k: (0, 0, 0))
    return pl.pallas_call(
        functools.partial(_ffn_kernel, sub=sub),
        out_shape=jax.ShapeDtypeStruct(x.shape, F32),
        grid=(bsz, t // tm, nk),
        in_specs=[
            pl.BlockSpec((None, tm, d), lambda bi, i, k: (bi, i, 0)),
            pl.BlockSpec((None, N_MOD, d), mod_map),
            pl.BlockSpec((d, tf), lambda bi, i, k: (0, k)),
            pl.BlockSpec((d, tf), lambda bi, i, k: (0, k + nk)),
            pl.BlockSpec((tf, d), lambda bi, i, k: (k, 0)),
            pl.BlockSpec((1, d), lambda bi, i, k: (0, 0)),
            pl.BlockSpec((1, d), lambda bi, i, k: (0, 0)),
        ],
        out_specs=pl.BlockSpec((None, tm, d), lambda bi, i, k: (bi, i, 0)),
        scratch_shapes=[pltpu.VMEM((tm, d), BF16), pltpu.VMEM((tm, d), F32)],
        compiler_params=_params("parallel", "parallel", "arbitrary"),
        name="ffn",
    )(x, mods, w13, w13, w2, g, b)


def _inproj_kernel(*refs, rope):
    if rope:
        x_ref, mod_ref, w_ref, cos_ref, sa_ref, sb_ref, o_ref, h_ref = refs
    else:
        x_ref, mod_ref, w_ref, o_ref, h_ref = refs
    j = pl.program_id(2)

    @pl.when(j == 0)
    def _():
        h_ref[...] = (x_ref[...] * (1.0 + mod_ref[4:5, :]) + mod_ref[3:4, :]).astype(BF16)

    y = jnp.dot(h_ref[...], w_ref[...], preferred_element_type=F32)
    if not rope:
        o_ref[...] = y
        return

    @pl.when(j == 0)
    def _():
        wide = lambda r: jnp.concatenate([r[...]] * (y.shape[1] // HEAD_W), axis=1)
        w = y.shape[1]
        o_ref[...] = (y * wide(cos_ref)
                      + pltpu.roll(y, w - QK_DIM // 4, 1) * wide(sa_ref)
                      + pltpu.roll(y, QK_DIM // 4, 1) * wide(sb_ref))

    @pl.when(j != 0)
    def _():
        o_ref[...] = y


def _inproj(x, mods, w_in, rope_tabs):
    bsz, t, d = x.shape
    tm = _row_tile(t)
    tn = 2 * GROUP_W
    rope = rope_tabs is not None
    per_batch = mods.shape[0] > 1
    mod_map = (lambda bi, i, j: (bi, 0, 0)) if per_batch else (lambda bi, i, j: (0, 0, 0))
    in_specs = [
        pl.BlockSpec((None, tm, d), lambda bi, i, j: (bi, i, 0)),
        pl.BlockSpec((None, N_MOD, d), mod_map),
        pl.BlockSpec((d, tn), lambda bi, i, j: (0, j)),
    ]
    args = [x, mods, w_in]
    if rope:
        in_specs += [pl.BlockSpec((tm, HEAD_W), lambda bi, i, j: (i, 0))] * 3
        args += list(rope_tabs)
    return pl.pallas_call(
        functools.partial(_inproj_kernel, rope=rope),
        out_shape=jax.ShapeDtypeStruct((bsz, t, EVEN_IN), F32),
        grid=(bsz, t // tm, EVEN_IN // tn),
        in_specs=in_specs,
        out_specs=pl.BlockSpec((None, tm, tn), lambda bi, i, j: (bi, i, j)),
        scratch_shapes=[pltpu.VMEM((tm, d), BF16)],
        compiler_params=_params("parallel", "parallel", "arbitrary"),
        name="inproj",
    )(*args)


def _rope_tables(n_tokens):
    pos = jnp.arange(n_tokens)
    half = QK_DIM // 2
    inv_freq = ROPE_BASE ** (-jnp.arange(0, half, 2, dtype=F32) / half)

    def table(p):
        ang = p.astype(F32)[:, None] * inv_freq[None, :]
        ang = jnp.concatenate([ang, ang], axis=-1)
        return jnp.cos(ang), jnp.sin(ang)

    (cos_r, sin_r), (cos_c, sin_c) = table(pos // GRID_W), table(pos % GRID_W)
    cos = jnp.concatenate([cos_r, cos_c, cos_r, cos_c], axis=-1)
    sin = jnp.concatenate([sin_r, sin_c, sin_r, sin_c], axis=-1)
    first = (jnp.arange(HEAD_W) % half) < half // 2
    return cos, jnp.where(first, -sin, 0.0), jnp.where(first, 0.0, sin)


def _attn_kernel(lam_ref, q_ref, *refs, n_src):
    kv_refs, o_ref = refs[:2 * n_src], refs[2 * n_src]
    q = q_ref[...] * (QK_DIM ** -0.5)
    lane = lax.broadcasted_iota(jnp.int32, (1, HEAD_W), 1)
    ks = [kv_refs[2 * i][...].astype(BF16) for i in range(n_src)]
    vs = [kv_refs[2 * i + 1][...].astype(BF16) for i in range(n_src)]
    probs = []
    for comp in range(2):
        sel = (lane < QK_DIM) if comp == 0 else (lane >= QK_DIM)
        qc = jnp.where(sel, q, 0.0).astype(BF16)
        ss = [lax.dot_general(qc, k, NT_DIMS, preferred_element_type=F32) for k in ks]
        m = functools.reduce(jnp.maximum, [jnp.max(s, axis=-1, keepdims=True) for s in ss])
        ps = [jnp.exp(s - m) for s in ss]
        tot = functools.reduce(jnp.add, [jnp.sum(p, axis=-1, keepdims=True) for p in ps])
        probs.append((ps, 1.0 / tot))
    lam = lam_ref[0]
    out = None
    for i in range(n_src):
        a = probs[0][0][i] * probs[0][1] - (lam * probs[1][1]) * probs[1][0][i]
        o = jnp.dot(a.astype(BF16), vs[i], preferred_element_type=F32)
        out = o if out is None else out + o
    o_ref[...] = out


def _attention(lam, p_q, kv_sources):
    bsz, t, _ = p_q.shape
    tq = _row_tile(t)
    n_src = len(kv_sources)
    in_specs = [
        pl.BlockSpec(memory_space=pltpu.SMEM),
        pl.BlockSpec((None, tq, HEAD_W), lambda bi, h, i: (bi, i, h)),
    ]
    args = [lam, p_q]
    for src in kv_sources:
        tk = src.shape[1]
        in_specs.append(pl.BlockSpec((None, tk, HEAD_W), lambda bi, h, i: (bi, 0, N_HEADS + h)))
        in_specs.append(pl.BlockSpec((None, tk, HEAD_W), lambda bi, h, i: (bi, 0, 2 * N_HEADS + h)))
        args += [src, src]
    return pl.pallas_call(
        functools.partial(_attn_kernel, n_src=n_src),
        out_shape=jax.ShapeDtypeStruct((bsz, t, GROUP_W), F32),
        grid=(bsz, N_HEADS, t // tq),
        in_specs=in_specs,
        out_specs=pl.BlockSpec((None, tq, HEAD_W), lambda bi, h, i: (bi, i, h)),
        compiler_params=_params("parallel", "parallel", "arbitrary"),
        name="diff_attn",
    )(*args)


def _time_cumsum(x, reverse):
    n = x.shape[0]
    row = lax.broadcasted_iota(jnp.int32, (n, 1), 0)
    d = 1
    while d < n:
        if reverse:
            x = x + jnp.where(row < n - d, pltpu.roll(x, n - d, 0), 0.0)
        else:
            x = x + jnp.where(row >= d, pltpu.roll(x, d, 0), 0.0)
        d *= 2
    return x


def _block_row(g, blk, idx):
    n, k = g.shape
    if blk >= 8:
        g4 = g.reshape(n // blk, blk // 8, 8, k)
        ref = g4[:, idx // 8:idx // 8 + 1, idx % 8:idx % 8 + 1, :]
        return jnp.broadcast_to(ref, g4.shape).reshape(n, k)
    g3 = g.reshape(n // 8, 8, k)
    sub = lax.broadcasted_iota(jnp.int32, (1, 8, 1), 1) // blk
    out = None
    for j in range(8 // blk):
        ref = jnp.broadcast_to(g3[:, j * blk + idx:j * blk + idx + 1, :], g3.shape)
        out = ref if out is None else jnp.where(sub == j, ref, out)
    return out.reshape(n, k)


def _hgrn_kernel(*refs, reverse, with_output):
    if with_output:
        q_ref, f_ref, i_ref, lb_ref, s0_ref, o_ref, s_ref = refs
    else:
        q_ref, f_ref, i_ref, lb_ref, s0_ref, s_ref = refs
    n = f_ref.shape[0]

    @pl.when(pl.program_id(1) == 0)
    def _():
        s_ref[...] = s0_ref[...]

    row = lax.broadcasted_iota(jnp.int32, (n, 1), 0)
    rt = lax.broadcasted_iota(jnp.int32, (n, n), 0)
    cs = lax.broadcasted_iota(jnp.int32, (n, n), 1)
    last = 0 if reverse else n - 1
    for h in range(N_HEADS):
        sl = slice(h * HEAD_W, (h + 1) * HEAD_W)
        lb = lb_ref[h:h + 1, :]
        f = lb + (1.0 - lb) * jax.nn.sigmoid(f_ref[:, sl])
        kk = 1.0 - f
        g = _time_cumsum(jnp.log(f), reverse)
        v = i_ref[:, sl].astype(BF16)
        st = s_ref[h]
        g_last = g[last:last + 1, :]
        if with_output:
            qq = _silu(q_ref[:, sl])
            att = jnp.where(rt == cs, lax.dot_general(qq.astype(BF16), kk.astype(BF16), NT_DIMS,
                                                      preferred_element_type=F32), 0.0)
            half = 1
            while half < n:
                blk = 2 * half
                pos = row & (blk - 1)
                is_q = (pos < half) if reverse else (pos >= half)
                d = g - _block_row(g, blk, half if reverse else half - 1)
                e = jnp.exp(jnp.where(is_q, d, -d))
                ql = jnp.where(is_q, qq * e, 0.0).astype(BF16)
                kl = jnp.where(is_q, 0.0, kk * e).astype(BF16)
                p = lax.dot_general(ql, kl, NT_DIMS, preferred_element_type=F32)
                shift = blk.bit_length() - 1
                att = att + jnp.where((rt >> shift) == (cs >> shift), p, 0.0)
                half = blk
            o = jnp.dot(att.astype(BF16), v, preferred_element_type=F32)
            o = o + lax.dot_general((qq * jnp.exp(g)).astype(BF16), st.astype(BF16), NT_DIMS,
                                    preferred_element_type=F32)
            o_ref[:, sl] = o
        kx = (kk * jnp.exp(g_last - g)).astype(BF16)
        s_ref[h] = jnp.exp(g_last) * st + lax.dot_general(v, kx, TN_DIMS, preferred_element_type=F32)


def _hgrn_scan(p, lb, s0, reverse, with_output):
    bsz, t, _ = p.shape
    c = HG_CHUNK
    nc = t // c
    cmap = (lambda ci: nc - 1 - ci) if reverse else (lambda ci: ci)
    col = lambda blk: (lambda bi, ci: (bi, cmap(ci), blk))
    state_spec = pl.BlockSpec((None, N_HEADS, HEAD_W, HEAD_W), lambda bi, ci: (bi, 0, 0, 0))
    out_shape = [jax.ShapeDtypeStruct((bsz, N_HEADS, HEAD_W, HEAD_W), F32)]
    out_specs = [state_spec]
    if with_output:
        out_shape.insert(0, jax.ShapeDtypeStruct((bsz, t, GROUP_W), F32))
        out_specs.insert(0, pl.BlockSpec((None, c, GROUP_W), col(0)))
    res = pl.pallas_call(
        functools.partial(_hgrn_kernel, reverse=reverse, with_output=with_output),
        out_shape=out_shape,
        grid=(bsz, nc),
        in_specs=[
            pl.BlockSpec((None, c, GROUP_W), col(3)),
            pl.BlockSpec((None, c, GROUP_W), col(5 if reverse else 4)),
            pl.BlockSpec((None, c, GROUP_W), col(6)),
            pl.BlockSpec((N_HEADS, HEAD_W), lambda bi, ci: (0, 0)),
            state_spec,
        ],
        out_specs=out_specs,
        compiler_params=_params("parallel", "arbitrary"),
        name="hgrn_bwd" if reverse else "hgrn_fwd",
    )(p, p, p, lb, s0)
    return (res[0], res[1]) if with_output else (None, res[0])


def _merge_kernel(x_ref, mod_ref, a_ref, of_ref, ob_ref, gt_ref, dn_ref, hn_ref, w_ref, g_ref, b_ref, o_ref,
                  *, da_scale):
    a = a_ref[...]
    o = of_ref[...] + ob_ref[...]
    parts_a, parts_o = [], []
    for h in range(N_HEADS):
        sl = slice(h * HEAD_W, (h + 1) * HEAD_W)
        parts_a.append(_rms_norm(a[:, sl], dn_ref[:, sl]) * da_scale)
        parts_o.append(_rms_norm(o[:, sl], hn_ref[:, sl]))
    ya = jnp.concatenate(parts_a, axis=1)
    yo = jnp.concatenate(parts_o, axis=1) * _silu(gt_ref[...])
    y = jnp.dot(jnp.concatenate([ya, yo], axis=1).astype(BF16), w_ref[...], preferred_element_type=F32)
    z = ALPHA * x_ref[...] + mod_ref[5:6, :] * y
    o_ref[...] = _layer_norm(z, g_ref[...], b_ref[...])


def _merge(x, mods, attn, o_f, o_b, p, da_norm, hg_norm, w_out, g, b, da_scale):
    bsz, t, d = x.shape
    tm = _row_tile(t)
    per_batch = mods.shape[0] > 1
    mod_map = (lambda bi, i: (bi, 0, 0)) if per_batch else (lambda bi, i: (0, 0, 0))
    grp = pl.BlockSpec((None, tm, GROUP_W), lambda bi, i: (bi, i, 0))
    vec = lambda n: pl.BlockSpec((1, n), lambda bi, i: (0, 0))
    return pl.pallas_call(
        functools.partial(_merge_kernel, da_scale=da_scale),
        out_shape=jax.ShapeDtypeStruct(x.shape, F32),
        grid=(bsz, t // tm),
        in_specs=[
            pl.BlockSpec((None, tm, d), lambda bi, i: (bi, i, 0)),
            pl.BlockSpec((None, N_MOD, d), mod_map),
            grp, grp, grp,
            pl.BlockSpec((None, tm, GROUP_W), lambda bi, i: (bi, i, 7)),
            vec(GROUP_W), vec(GROUP_W),
            pl.BlockSpec((d, d), lambda bi, i: (0, 0)),
            vec(d), vec(d),
        ],
        out_specs=pl.BlockSpec((None, tm, d), lambda bi, i: (bi, i, 0)),
        compiler_params=_params("parallel", "parallel"),
        name="even_merge",
    )(x, mods, attn, o_f, o_b, p, da_norm, hg_norm, w_out, g, b)


def _glu_kernel(x_ref, mod_ref, w_ref, b_ref, o_ref):
    h = (x_ref[...] * (1.0 + mod_ref[4:5, :]) + mod_ref[3:4, :]).astype(BF16)
    y = jnp.dot(h, w_ref[...], preferred_element_type=F32) + b_ref[...]
    d = o_ref.shape[-1]
    o_ref[...] = y[:, :d] * jax.nn.sigmoid(y[:, d:])


def _glu(x, mods, w_pw1, b_pw1):
    bsz, t, d = x.shape
    tm = _row_tile(t)
    per_batch = mods.shape[0] > 1
    mod_map = (lambda bi, i: (bi, 0, 0)) if per_batch else (lambda bi, i: (0, 0, 0))
    return pl.pallas_call(
        _glu_kernel,
        out_shape=jax.ShapeDtypeStruct(x.shape, F32),
        grid=(bsz, t // tm),
        in_specs=[
            pl.BlockSpec((None, tm, d), lambda bi, i: (bi, i, 0)),
            pl.BlockSpec((None, N_MOD, d), mod_map),
            pl.BlockSpec((d, 2 * d), lambda bi, i: (0, 0)),
            pl.BlockSpec((1, 2 * d), lambda bi, i: (0, 0)),
        ],
        out_specs=pl.BlockSpec((None, tm, d), lambda bi, i: (bi, i, 0)),
        compiler_params=_params("parallel", "parallel"),
        name="conv_glu",
    )(x, mods, w_pw1, b_pw1)


def _conv_kernel(x_ref, mod_ref, u_ref, up_ref, un_ref, wdw_ref, bdw_ref, cg_ref, cb_ref, w_ref, b2_ref,
                 g_ref, b_ref, o_ref, ext_ref):
    i = pl.program_id(1)
    tm = u_ref.shape[0]
    ext_ref[0:CONV_HALO, :] = jnp.where(i > 0, up_ref[...], 0.0)
    ext_ref[CONV_HALO:CONV_HALO + tm, :] = u_ref[...]
    ext_ref[CONV_HALO + tm:, :] = jnp.where(i < pl.num_programs(1) - 1, un_ref[...], 0.0)
    off = CONV_HALO - CONV_WIDTH // 2
    acc = None
    for w in range(CONV_WIDTH):
        term = ext_ref[off + w:off + w + tm, :] * wdw_ref[w:w + 1, :]
        acc = term if acc is None else acc + term
    u = _silu(_layer_norm(acc + bdw_ref[...], cg_ref[...], cb_ref[...]))
    y = jnp.dot(u.astype(BF16), w_ref[...], preferred_element_type=F32) + b2_ref[...]
    z = ALPHA * x_ref[...] + mod_ref[5:6, :] * y
    o_ref[...] = _layer_norm(z, g_ref[...], b_ref[...])


def _conv(x, mods, u, w_dw, b_dw, cg, cb, w_pw2, b_pw2, g, b):
    bsz, t, d = x.shape
    tm = _row_tile(t)
    per_batch = mods.shape[0] > 1
    mod_map = (lambda bi, i: (bi, 0, 0)) if per_batch else (lambda bi, i: (0, 0, 0))
    r = tm // CONV_HALO
    n_halo = t // CONV_HALO
    row = pl.BlockSpec((None, tm, d), lambda bi, i: (bi, i, 0))
    vec = pl.BlockSpec((1, d), lambda bi, i: (0, 0))
    return pl.pallas_call(
        _conv_kernel,
        out_shape=jax.ShapeDtypeStruct(x.shape, F32),
        grid=(bsz, t // tm),
        in_specs=[
            row,
            pl.BlockSpec((None, N_MOD, d), mod_map),
            row,
            pl.BlockSpec((None, CONV_HALO, d), lambda bi, i: (bi, jnp.maximum(i * r - 1, 0), 0)),
            pl.BlockSpec((None, CONV_HALO, d), lambda bi, i: (bi, jnp.minimum((i + 1) * r, n_halo - 1), 0)),
            pl.BlockSpec((CONV_WIDTH, d), lambda bi, i: (0, 0)),
            vec, vec, vec,
            pl.BlockSpec((d, d), lambda bi, i: (0, 0)),
            vec, vec, vec,
        ],
        out_specs=row,
        scratch_shapes=[pltpu.VMEM((tm + 2 * CONV_HALO, d), F32)],
        compiler_params=_params("parallel", "parallel"),
        name="conv_mix",
    )(x, mods, u, u, u, w_dw, b_dw, cg, cb, w_pw2, b_pw2, g, b)


def kernel(x, c, ctx, c_ctx, w_mod, b_mod, ln_g, ln_b, ffn_w13, ffn_w2, ev_w_in, ev_w_out, da_lambda, da_norm,
           hg_lb, hg_norm, cv_w_pw1, cv_b_pw1, cv_w_dw, cv_b_dw, cv_ln_g, cv_ln_b, cv_w_pw2, cv_b_pw2):
    bsz, seq, d = x.shape
    row = lambda a: a.reshape(1, -1)

    cc = jnp.zeros((16, d), F32).at[:bsz].set(c).at[bsz].set(c_ctx)
    mods_all = _mods(cc, w_mod, b_mod)
    rope_tabs = _rope_tables(seq)
    lb_w = jax.nn.softmax(hg_lb.astype(F32), axis=0)
    lower_bounds = jnp.cumsum(lb_w, axis=0) - lb_w[:1]

    xl, xc = x, ctx
    for layer in range(DEPTH):
        j = layer // 2
        even = layer % 2 == 0
        ctx_pre = layer <= LAST_CTX_READ
        ctx_full = layer < LAST_CTX_READ
        mods_l = mods_all[layer, :bsz].reshape(bsz, N_MOD, d)
        mods_c = mods_all[layer, bsz:bsz + 1].reshape(1, N_MOD, d)
        w13 = ffn_w13[layer].astype(BF16)
        w2 = ffn_w2[layer].astype(BF16)
        ln = lambda s: (row(ln_g[layer, s]), row(ln_b[layer, s]))

        xl = _ffn(xl, mods_l, w13[0], w2[0], *ln(0), sub=0)
        if ctx_pre:
            xc = _ffn(xc, mods_c, w13[0], w2[0], *ln(0), sub=0)

        if even:
            lam_init = 0.8 - 0.6 * math.exp(-0.3 * layer)
            lv = da_lambda[j].astype(F32)
            lam = (jnp.exp(jnp.sum(lv[0] * lv[1])) - jnp.exp(jnp.sum(lv[2] * lv[3])) + lam_init).reshape(1)
            lb_f = lower_bounds[j, :GROUP_W].reshape(N_HEADS, HEAD_W)
            lb_b = lower_bounds[j, GROUP_W:].reshape(N_HEADS, HEAD_W)
            w_in = ev_w_in[j].astype(BF16)
            w_out = ev_w_out[j].astype(BF16)
            p_l = _inproj(xl, mods_l, w_in, rope_tabs)
            p_c = _inproj(xc, mods_c, w_in, None)
            a_l = _attention(lam, p_l, [p_l, p_c])
            zeros = jnp.zeros((bsz, N_HEADS, HEAD_W, HEAD_W), F32)
            ocf, s_f = _hgrn_scan(p_c, lb_f, zeros, False, ctx_full)
            ocb, s_b = _hgrn_scan(p_c, lb_b, zeros, True, ctx_full)
            olf, _ = _hgrn_scan(p_l, lb_f, s_f, False, True)
            olb, _ = _hgrn_scan(p_l, lb_b, s_b, True, True)
            merge_args = (row(da_norm[j]), row(hg_norm[j]), w_out, *ln(1), 1.0 - lam_init)
            xl_new = _merge(xl, mods_l, a_l, olf, olb, p_l, *merge_args)
            if ctx_full:
                a_c = _attention(lam, p_c, [p_c])
                xc = _merge(xc, mods_c, a_c, ocf, ocb, p_c, *merge_args)
            xl = xl_new
        else:
            w_pw1 = cv_w_pw1[j].astype(BF16)
            w_pw2 = cv_w_pw2[j].astype(BF16)
            conv_args = (cv_w_dw[j], row(cv_b_dw[j]), row(cv_ln_g[j]), row(cv_ln_b[j]), w_pw2, row(cv_b_pw2[j]),
                         *ln(1))
            ul = _glu(xl, mods_l, w_pw1, row(cv_b_pw1[j]))
            if ctx_full:
                uc = _glu(xc, mods_c, w_pw1, row(cv_b_pw1[j]))
                xc = _conv(xc, mods_c, uc, *conv_args)
            xl = _conv(xl, mods_l, ul, *conv_args)

        xl = _ffn(xl, mods_l, w13[1], w2[1], *ln(2), sub=2)
        if ctx_full:
            xc = _ffn(xc, mods_c, w13[1], w2[1], *ln(2), sub=2)
    return xl
```

```python
import functools
import math

import jax
import jax.numpy as jnp
from jax import lax
from jax.experimental import pallas as pl
from jax.experimental.pallas import tpu as pltpu

F32 = jnp.float32
BF16 = jnp.bfloat16

D_MODEL = 1024
DEPTH = 4
GRID_W = 64
N_HEADS = 4
HEAD_W = 128
QK_DIM = 64
GROUP_W = N_HEADS * HEAD_W
D_FF = 2816
N_MOD = 9
CONV_WIDTH = 31
CONV_HALO = 16
ROPE_BASE = 10000.0
LAST_CTX_READ = 2 * ((DEPTH - 1) // 2)
ALPHA = (2 * DEPTH) ** 0.25
EPS = 1e-5
HG_CHUNK = 128
FFN_CHUNK = D_FF // 2
ROW_TILE = 512
VMEM_LIMIT = 56 * 1024 * 1024

PB_WIDTH = 6 * GROUP_W
PF_WIDTH = 2 * GROUP_W
PB_Q, PB_K, PB_V, PB_HQ, PB_HI, PB_HG = range(6)

NT_DIMS = (((1,), (1,)), ((), ()))
TN_DIMS = (((0,), (0,)), ((), ()))


def _params(*sem):
    return pltpu.CompilerParams(dimension_semantics=sem, vmem_limit_bytes=VMEM_LIMIT)


def _layer_norm(z, g, b):
    mu = jnp.mean(z, axis=-1, keepdims=True)
    zc = z - mu
    var = jnp.mean(zc * zc, axis=-1, keepdims=True)
    return zc * lax.rsqrt(var + EPS) * g + b


def _rms_norm(z, g):
    return z * lax.rsqrt(jnp.mean(z * z, axis=-1, keepdims=True) + EPS) * g


def _silu(a):
    return a * jax.nn.sigmoid(a)


def _row_tile(t):
    return min(t, ROW_TILE)


def _mod_spec(layer, mod_row, grid_rank):
    if grid_rank == 2:
        return pl.BlockSpec((None, None, N_MOD, D_MODEL), lambda bi, i: (layer, mod_row(bi), 0, 0))
    return pl.BlockSpec((None, None, N_MOD, D_MODEL), lambda bi, i, j: (layer, mod_row(bi), 0, 0))


def _mods_kernel(c_ref, w_ref, b_ref, o_ref):
    a = _silu(c_ref[...]).astype(BF16)
    o_ref[...] = jnp.dot(a, w_ref[...].astype(BF16), preferred_element_type=F32) + b_ref[...]


def _mods(cc, w_mod, b_mod):
    n = cc.shape[0]
    tn = 1024
    return pl.pallas_call(
        _mods_kernel,
        out_shape=jax.ShapeDtypeStruct((DEPTH, n, N_MOD * D_MODEL), F32),
        grid=(DEPTH, N_MOD * D_MODEL // tn),
        in_specs=[
            pl.BlockSpec((n, D_MODEL), lambda l, j: (0, 0)),
            pl.BlockSpec((None, D_MODEL, tn), lambda l, j: (l, 0, j)),
            pl.BlockSpec((None, 1, tn), lambda l, j: (l, 0, j)),
        ],
        out_specs=pl.BlockSpec((None, n, tn), lambda l, j: (l, 0, j)),
        compiler_params=_params("parallel", "parallel"),
        name="mods",
    )(cc, w_mod, b_mod.reshape(DEPTH, 1, N_MOD * D_MODEL))


def _ffn_kernel(xn_ref, xp_ref, modn_ref, modp_ref, w13_ref, w2_ref, g_ref, b_ref, o_ref, h_scr, acc_scr,
                *, sub, n_tiles):
    s = pl.program_id(0)

    def prologue(dst):
        shift = modn_ref[3 * sub:3 * sub + 1, :]
        scale = modn_ref[3 * sub + 1:3 * sub + 2, :]
        h_scr[dst] = (xn_ref[...] * (1.0 + scale) + shift).astype(BF16)

    def epilogue(src):
        gate = modp_ref[3 * sub + 2:3 * sub + 3, :]
        z = ALPHA * xp_ref[...] + gate * (0.5 * acc_scr[src])
        o_ref[...] = _layer_norm(z, g_ref[sub:sub + 1, :], b_ref[sub:sub + 1, :])

    def matmuls(slot):
        h = h_scr[slot]
        for fc in range(D_FF // FFN_CHUNK):
            lo = fc * FFN_CHUNK
            a = jnp.dot(h, w13_ref[:, lo:lo + FFN_CHUNK], preferred_element_type=F32)
            b = jnp.dot(h, w13_ref[:, D_FF + lo:D_FF + lo + FFN_CHUNK], preferred_element_type=F32)
            y = jnp.dot((_silu(a) * b).astype(BF16), w2_ref[lo:lo + FFN_CHUNK, :], preferred_element_type=F32)
            if fc == 0:
                acc_scr[slot] = y
            else:
                acc_scr[slot] += y

    @pl.when(s == 0)
    def _():
        prologue(0)
        acc_scr[1] = jnp.zeros(acc_scr.shape[1:], F32)

    for parity in range(2):
        @pl.when((s >= 1) & (s <= n_tiles) & (s % 2 == parity))
        def _():
            epilogue(parity)
            prologue(parity)
            matmuls(1 - parity)

    @pl.when(s == n_tiles + 1)
    def _():
        epilogue((n_tiles + 1) % 2)


def _ffn(x, mods, layer, mod_row, w13, w2, fi, ln_g, ln_b, sub):
    bsz, t, d = x.shape
    tm = _row_tile(t)
    per_b = t // tm
    n = bsz * per_b
    xf = x.reshape(bsz * t, d)
    nxt = lambda s: jnp.minimum(s, n - 1)
    prv = lambda s: jnp.clip(s - 2, 0, n - 1)
    resident = dict(pipeline_mode=pl.Buffered(1))
    out = pl.pallas_call(
        functools.partial(_ffn_kernel, sub=sub, n_tiles=n),
        out_shape=jax.ShapeDtypeStruct(xf.shape, F32),
        grid=(n + 2,),
        in_specs=[
            pl.BlockSpec((tm, d), lambda s: (nxt(s), 0)),
            pl.BlockSpec((tm, d), lambda s: (prv(s), 0)),
            pl.BlockSpec((None, None, N_MOD, d), lambda s: (layer, mod_row(nxt(s) // per_b), 0, 0)),
            pl.BlockSpec((None, None, N_MOD, d), lambda s: (layer, mod_row(prv(s) // per_b), 0, 0)),
            pl.BlockSpec((None, None, d, 2 * D_FF), lambda s: (layer, fi, 0, 0), **resident),
            pl.BlockSpec((None, None, D_FF, d), lambda s: (layer, fi, 0, 0), **resident),
            pl.BlockSpec((None, 3, d), lambda s: (layer, 0, 0)),
            pl.BlockSpec((None, 3, d), lambda s: (layer, 0, 0)),
        ],
        out_specs=pl.BlockSpec((tm, d), lambda s: (prv(s), 0)),
        scratch_shapes=[pltpu.VMEM((2, tm, d), BF16), pltpu.VMEM((2, tm, d), F32)],
        compiler_params=_params("arbitrary"),
        name="ffn",
    )(xf, xf, mods, mods, w13, w2, ln_g, ln_b)
    return out.reshape(x.shape)


def _inproj_kernel(*refs, rope):
    if rope:
        x_ref, mod_ref, w_ref, cos_ref, sa_ref, sb_ref, ob_ref, of_ref = refs
    else:
        x_ref, mod_ref, w_ref, ob_ref, of_ref = refs
    h = (x_ref[...] * (1.0 + mod_ref[4:5, :]) + mod_ref[3:4, :]).astype(BF16)
    w2 = 2 * GROUP_W
    proj = lambda grp: jnp.dot(h, w_ref[:, grp * w2:(grp + 1) * w2], preferred_element_type=F32)
    y = proj(0)
    if rope:
        wide = lambda r: jnp.concatenate([r[...]] * (w2 // HEAD_W), axis=1)
        y = (y * wide(cos_ref) + pltpu.roll(y, w2 - QK_DIM // 4, 1) * wide(sa_ref)
             + pltpu.roll(y, QK_DIM // 4, 1) * wide(sb_ref))
    ob_ref[:, 0:w2] = y.astype(BF16)
    ob_ref[:, w2:2 * w2] = proj(1).astype(BF16)
    of_ref[...] = proj(2)
    ob_ref[:, 2 * w2:3 * w2] = proj(3).astype(BF16)


def _inproj(x, mods, layer, mod_row, w_in, j, rope_tabs):
    bsz, t, d = x.shape
    tm = _row_tile(t)
    rope = rope_tabs is not None
    in_specs = [
        pl.BlockSpec((None, tm, d), lambda bi, i: (bi, i, 0)),
        _mod_spec(layer, mod_row, 2),
        pl.BlockSpec((None, d, 8 * GROUP_W), lambda bi, i: (j, 0, 0)),
    ]
    args = [x, mods, w_in]
    if rope:
        in_specs += [pl.BlockSpec((tm, HEAD_W), lambda bi, i: (i, 0))] * 3
        args += list(rope_tabs)
    return pl.pallas_call(
        functools.partial(_inproj_kernel, rope=rope),
        out_shape=[jax.ShapeDtypeStruct((bsz, t, PB_WIDTH), BF16), jax.ShapeDtypeStruct((bsz, t, PF_WIDTH), F32)],
        grid=(bsz, t // tm),
        in_specs=in_specs,
        out_specs=[pl.BlockSpec((None, tm, PB_WIDTH), lambda bi, i: (bi, i, 0)),
                   pl.BlockSpec((None, tm, PF_WIDTH), lambda bi, i: (bi, i, 0))],
        compiler_params=_params("parallel", "parallel"),
        name="inproj",
    )(*args)


def _rope_tables(n_tokens):
    pos = jnp.arange(n_tokens)
    half = QK_DIM // 2
    inv_freq = ROPE_BASE ** (-jnp.arange(0, half, 2, dtype=F32) / half)

    def table(p):
        ang = p.astype(F32)[:, None] * inv_freq[None, :]
        ang = jnp.concatenate([ang, ang], axis=-1)
        return jnp.cos(ang), jnp.sin(ang)

    (cos_r, sin_r), (cos_c, sin_c) = table(pos // GRID_W), table(pos % GRID_W)
    cos = jnp.concatenate([cos_r, cos_c, cos_r, cos_c], axis=-1)
    sin = jnp.concatenate([sin_r, sin_c, sin_r, sin_c], axis=-1)
    first = (jnp.arange(HEAD_W) % half) < half // 2
    return cos, jnp.where(first, -sin, 0.0), jnp.where(first, 0.0, sin)


def _attn_kernel(lam_ref, q_ref, *refs, n_src):
    kv_refs, o_ref = refs[:2 * n_src], refs[2 * n_src]
    q = q_ref[...] * (QK_DIM ** -0.5)
    lane = lax.broadcasted_iota(jnp.int32, (1, HEAD_W), 1)
    ks = [kv_refs[2 * i][...] for i in range(n_src)]
    vs = [jnp.concatenate([kv_refs[2 * i + 1][...], jnp.ones(kv_refs[2 * i + 1].shape, BF16)], axis=1)
          for i in range(n_src)]
    mixed = []
    for comp in range(2):
        sel = (lane < QK_DIM) if comp == 0 else (lane >= QK_DIM)
        qc = jnp.where(sel, q, jnp.zeros_like(q))
        ss = [lax.dot_general(qc, k, NT_DIMS, preferred_element_type=F32) for k in ks]
        m = functools.reduce(jnp.maximum, [jnp.max(s, axis=-1, keepdims=True) for s in ss])
        acc = None
        for s, v in zip(ss, vs):
            p = jnp.exp((s - m).astype(BF16))
            o = jnp.dot(p, v, preferred_element_type=F32)
            acc = o if acc is None else acc + o
        mixed.append(acc[:, :HEAD_W] / acc[:, HEAD_W:HEAD_W + 1])
    o_ref[...] = mixed[0] - lam_ref[0] * mixed[1]


def _attention(lam, p_q, kv_sources):
    bsz, t, _ = p_q.shape
    tq = _row_tile(t)
    n_src = len(kv_sources)
    in_specs = [
        pl.BlockSpec(memory_space=pltpu.SMEM),
        pl.BlockSpec((None, tq, HEAD_W), lambda bi, h, i: (bi, i, PB_Q * N_HEADS + h)),
    ]
    args = [lam, p_q]
    for src in kv_sources:
        tk = src.shape[1]
        in_specs.append(pl.BlockSpec((None, tk, HEAD_W), lambda bi, h, i: (bi, 0, PB_K * N_HEADS + h)))
        in_specs.append(pl.BlockSpec((None, tk, HEAD_W), lambda bi, h, i: (bi, 0, PB_V * N_HEADS + h)))
        args += [src, src]
    return pl.pallas_call(
        functools.partial(_attn_kernel, n_src=n_src),
        out_shape=jax.ShapeDtypeStruct((bsz, t, GROUP_W), F32),
        grid=(bsz, N_HEADS, t // tq),
        in_specs=in_specs,
        out_specs=pl.BlockSpec((None, tq, HEAD_W), lambda bi, h, i: (bi, i, h)),
        compiler_params=_params("parallel", "parallel", "arbitrary"),
        name="diff_attn",
    )(*args)


def _time_cumsum(x, reverse):
    n = x.shape[0]
    row = lax.broadcasted_iota(jnp.int32, (n, 1), 0)
    d = 1
    while d < n:
        if reverse:
            x = x + jnp.where(row < n - d, pltpu.roll(x, n - d, 0), 0.0)
        else:
            x = x + jnp.where(row >= d, pltpu.roll(x, d, 0), 0.0)
        d *= 2
    return x


def _block_row(g, blk, idx):
    n, k = g.shape
    if blk >= 8:
        g4 = g.reshape(n // blk, blk // 8, 8, k)
        ref = g4[:, idx // 8:idx // 8 + 1, idx % 8:idx % 8 + 1, :]
        return jnp.broadcast_to(ref, g4.shape).reshape(n, k)
    g3 = g.reshape(n // 8, 8, k)
    sub = lax.broadcasted_iota(jnp.int32, (1, 8, 1), 1) // blk
    out = None
    for j in range(8 // blk):
        ref = jnp.broadcast_to(g3[:, j * blk + idx:j * blk + idx + 1, :], g3.shape)
        out = ref if out is None else jnp.where(sub == j, ref, out)
    return out.reshape(n, k)


def _hgrn_kernel(*refs, reverse, with_output):
    if with_output:
        q_ref, f_ref, i_ref, lb_ref, s0_ref, o_ref, s_ref = refs
    else:
        q_ref, f_ref, i_ref, lb_ref, s0_ref, s_ref = refs
    n = f_ref.shape[0]

    @pl.when(pl.program_id(1) == 0)
    def _():
        s_ref[...] = s0_ref[...]

    row = lax.broadcasted_iota(jnp.int32, (n, 1), 0)
    rt = lax.broadcasted_iota(jnp.int32, (n, n), 0)
    cs = lax.broadcasted_iota(jnp.int32, (n, n), 1)
    last = 0 if reverse else n - 1
    for h in range(N_HEADS):
        sl = slice(h * HEAD_W, (h + 1) * HEAD_W)
        lb = lb_ref[h:h + 1, :]
        f = lb + (1.0 - lb) * jax.nn.sigmoid(f_ref[:, sl])
        kk = 1.0 - f
        g = _time_cumsum(jnp.log(f), reverse)
        v = i_ref[:, sl]
        st = s_ref[h]
        g_last = g[last:last + 1, :]
        if with_output:
            qq = _silu(q_ref[:, sl].astype(F32))
            att = jnp.where(rt == cs, lax.dot_general(qq.astype(BF16), kk.astype(BF16), NT_DIMS,
                                                      preferred_element_type=F32), 0.0)
            half = 1
            while half < n:
                blk = 2 * half
                pos = row & (blk - 1)
                is_q = (pos < half) if reverse else (pos >= half)
                d = g - _block_row(g, blk, half if reverse else half - 1)
                e = jnp.exp(jnp.where(is_q, d, -d))
                ql = jnp.where(is_q, qq * e, 0.0).astype(BF16)
                kl = jnp.where(is_q, 0.0, kk * e).astype(BF16)
                p = lax.dot_general(ql, kl, NT_DIMS, preferred_element_type=F32)
                shift = blk.bit_length() - 1
                att = att + jnp.where((rt >> shift) == (cs >> shift), p, 0.0)
                half = blk
            o = jnp.dot(att.astype(BF16), v, preferred_element_type=F32)
            o = o + lax.dot_general((qq * jnp.exp(g)).astype(BF16), st.astype(BF16), NT_DIMS,
                                    preferred_element_type=F32)
            o_ref[:, sl] = o
        kx = (kk * jnp.exp(g_last - g)).astype(BF16)
        s_ref[h] = jnp.exp(g_last) * st + lax.dot_general(v, kx, TN_DIMS, preferred_element_type=F32)


def _hgrn_scan(p_b, p_f, lb, s0, reverse, with_output):
    bsz, t, _ = p_b.shape
    c = HG_CHUNK
    nc = t // c
    cmap = (lambda ci: nc - 1 - ci) if reverse else (lambda ci: ci)
    col = lambda blk: (lambda bi, ci: (bi, cmap(ci), blk))
    state_spec = pl.BlockSpec((None, N_HEADS, HEAD_W, HEAD_W), lambda bi, ci: (bi, 0, 0, 0))
    out_shape = [jax.ShapeDtypeStruct((bsz, N_HEADS, HEAD_W, HEAD_W), F32)]
    out_specs = [state_spec]
    if with_output:
        out_shape.insert(0, jax.ShapeDtypeStruct((bsz, t, GROUP_W), F32))
        out_specs.insert(0, pl.BlockSpec((None, c, GROUP_W), col(0)))
    res = pl.pallas_call(
        functools.partial(_hgrn_kernel, reverse=reverse, with_output=with_output),
        out_shape=out_shape,
        grid=(bsz, nc),
        in_specs=[
            pl.BlockSpec((None, c, GROUP_W), col(PB_HQ)),
            pl.BlockSpec((None, c, GROUP_W), col(1 if reverse else 0)),
            pl.BlockSpec((None, c, GROUP_W), col(PB_HI)),
            pl.BlockSpec((N_HEADS, HEAD_W), lambda bi, ci: (0, 0)),
            state_spec,
        ],
        out_specs=out_specs,
        compiler_params=_params("parallel", "arbitrary"),
        name="hgrn_bwd" if reverse else "hgrn_fwd",
    )(p_b, p_f, p_b, lb, s0)
    return (res[0], res[1]) if with_output else (None, res[0])


def _merge_kernel(x_ref, mod_ref, a_ref, of_ref, ob_ref, gt_ref, dn_ref, hn_ref, w_ref, g_ref, b_ref, o_ref,
                  *, da_scale):
    a = a_ref[...]
    o = of_ref[...] + ob_ref[...]
    parts_a, parts_o = [], []
    for h in range(N_HEADS):
        sl = slice(h * HEAD_W, (h + 1) * HEAD_W)
        parts_a.append(_rms_norm(a[:, sl], dn_ref[:, sl]) * da_scale)
        parts_o.append(_rms_norm(o[:, sl], hn_ref[:, sl]))
    ya = jnp.concatenate(parts_a, axis=1)
    yo = jnp.concatenate(parts_o, axis=1) * _silu(gt_ref[...].astype(F32))
    y = jnp.dot(jnp.concatenate([ya, yo], axis=1).astype(BF16), w_ref[...], preferred_element_type=F32)
    z = ALPHA * x_ref[...] + mod_ref[5:6, :] * y
    o_ref[...] = _layer_norm(z, g_ref[1:2, :], b_ref[1:2, :])


def _merge(x, mods, layer, mod_row, attn, o_f, o_b, p_b, da_norm, hg_norm, w_out, j, ln_g, ln_b, da_scale):
    bsz, t, d = x.shape
    tm = _row_tile(t)
    grp = pl.BlockSpec((None, tm, GROUP_W), lambda bi, i: (bi, i, 0))
    vec = lambda n: pl.BlockSpec((None, 1, n), lambda bi, i: (j, 0, 0))
    ln = pl.BlockSpec((None, 3, d), lambda bi, i: (layer, 0, 0))
    return pl.pallas_call(
        functools.partial(_merge_kernel, da_scale=da_scale),
        out_shape=jax.ShapeDtypeStruct(x.shape, F32),
        grid=(bsz, t // tm),
        in_specs=[
            pl.BlockSpec((None, tm, d), lambda bi, i: (bi, i, 0)),
            _mod_spec(layer, mod_row, 2),
            grp, grp, grp,
            pl.BlockSpec((None, tm, GROUP_W), lambda bi, i: (bi, i, PB_HG)),
            vec(GROUP_W), vec(GROUP_W),
            pl.BlockSpec((None, d, d), lambda bi, i: (j, 0, 0)),
            ln, ln,
        ],
        out_specs=pl.BlockSpec((None, tm, d), lambda bi, i: (bi, i, 0)),
        compiler_params=_params("parallel", "parallel"),
        name="even_merge",
    )(x, mods, attn, o_f, o_b, p_b, da_norm, hg_norm, w_out, ln_g, ln_b)


def _glu_kernel(x_ref, mod_ref, w_ref, b_ref, o_ref):
    h = (x_ref[...] * (1.0 + mod_ref[4:5, :]) + mod_ref[3:4, :]).astype(BF16)
    y = jnp.dot(h, w_ref[...], preferred_element_type=F32) + b_ref[...]
    d = o_ref.shape[-1]
    o_ref[...] = y[:, :d] * jax.nn.sigmoid(y[:, d:])


def _glu(x, mods, layer, mod_row, w_pw1, b_pw1, j):
    bsz, t, d = x.shape
    tm = _row_tile(t)
    return pl.pallas_call(
        _glu_kernel,
        out_shape=jax.ShapeDtypeStruct(x.shape, F32),
        grid=(bsz, t // tm),
        in_specs=[
            pl.BlockSpec((None, tm, d), lambda bi, i: (bi, i, 0)),
            _mod_spec(layer, mod_row, 2),
            pl.BlockSpec((None, d, 2 * d), lambda bi, i: (j, 0, 0)),
            pl.BlockSpec((None, 1, 2 * d), lambda bi, i: (j, 0, 0)),
        ],
        out_specs=pl.BlockSpec((None, tm, d), lambda bi, i: (bi, i, 0)),
        compiler_params=_params("parallel", "parallel"),
        name="conv_glu",
    )(x, mods, w_pw1, b_pw1)


def _conv_kernel(x_ref, mod_ref, u_ref, up_ref, un_ref, wdw_ref, bdw_ref, cg_ref, cb_ref, w_ref, b2_ref,
                 g_ref, b_ref, o_ref, ext_ref):
    i = pl.program_id(1)
    tm = u_ref.shape[0]
    ext_ref[0:CONV_HALO, :] = jnp.where(i > 0, up_ref[...], 0.0)
    ext_ref[CONV_HALO:CONV_HALO + tm, :] = u_ref[...]
    ext_ref[CONV_HALO + tm:, :] = jnp.where(i < pl.num_programs(1) - 1, un_ref[...], 0.0)
    off = CONV_HALO - CONV_WIDTH // 2
    acc = None
    for w in range(CONV_WIDTH):
        term = ext_ref[off + w:off + w + tm, :] * wdw_ref[w:w + 1, :]
        acc = term if acc is None else acc + term
    u = _silu(_layer_norm(acc + bdw_ref[...], cg_ref[...], cb_ref[...]))
    y = jnp.dot(u.astype(BF16), w_ref[...], preferred_element_type=F32) + b2_ref[...]
    z = ALPHA * x_ref[...] + mod_ref[5:6, :] * y
    o_ref[...] = _layer_norm(z, g_ref[1:2, :], b_ref[1:2, :])


def _conv(x, mods, layer, mod_row, u, w_dw, b_dw, cg, cb, w_pw2, b_pw2, j, ln_g, ln_b):
    bsz, t, d = x.shape
    tm = _row_tile(t)
    r = tm // CONV_HALO
    n_halo = t // CONV_HALO
    row = pl.BlockSpec((None, tm, d), lambda bi, i: (bi, i, 0))
    vec = pl.BlockSpec((None, 1, d), lambda bi, i: (j, 0, 0))
    ln = pl.BlockSpec((None, 3, d), lambda bi, i: (layer, 0, 0))
    return pl.pallas_call(
        _conv_kernel,
        out_shape=jax.ShapeDtypeStruct(x.shape, F32),
        grid=(bsz, t // tm),
        in_specs=[
            row,
            _mod_spec(layer, mod_row, 2),
            row,
            pl.BlockSpec((None, CONV_HALO, d), lambda bi, i: (bi, jnp.maximum(i * r - 1, 0), 0)),
            pl.BlockSpec((None, CONV_HALO, d), lambda bi, i: (bi, jnp.minimum((i + 1) * r, n_halo - 1), 0)),
            pl.BlockSpec((None, CONV_WIDTH, d), lambda bi, i: (j, 0, 0)),
            vec, vec, vec,
            pl.BlockSpec((None, d, d), lambda bi, i: (j, 0, 0)),
            vec, ln, ln,
        ],
        out_specs=row,
        scratch_shapes=[pltpu.VMEM((tm + 2 * CONV_HALO, d), F32)],
        compiler_params=_params("parallel", "parallel"),
        name="conv_mix",
    )(x, mods, u, u, u, w_dw, b_dw, cg, cb, w_pw2, b_pw2, ln_g, ln_b)


def kernel(x, c, ctx, c_ctx, w_mod, b_mod, ln_g, ln_b, ffn_w13, ffn_w2, ev_w_in, ev_w_out, da_lambda, da_norm,
           hg_lb, hg_norm, cv_w_pw1, cv_b_pw1, cv_w_dw, cv_b_dw, cv_ln_g, cv_ln_b, cv_w_pw2, cv_b_pw2):
    bsz, seq, d = x.shape
    per_layer_row = lambda a: a.reshape(a.shape[0], 1, a.shape[1])

    cc = jnp.zeros((16, d), F32).at[:bsz].set(c).at[bsz].set(c_ctx)
    mods = _mods(cc, w_mod, b_mod).reshape(DEPTH, 16, N_MOD, d)
    lat_row = lambda bi: bi
    ctx_row = lambda bi: bsz
    rope_tabs = _rope_tables(seq)
    lb_w = jax.nn.softmax(hg_lb.astype(F32), axis=0)
    lower_bounds = jnp.cumsum(lb_w, axis=0) - lb_w[:1]

    w13, w2 = ffn_w13.astype(BF16), ffn_w2.astype(BF16)
    w_in, w_out = ev_w_in.astype(BF16), ev_w_out.astype(BF16)
    w_pw1, w_pw2 = cv_w_pw1.astype(BF16), cv_w_pw2.astype(BF16)
    da_norm3, hg_norm3 = per_layer_row(da_norm), per_layer_row(hg_norm)
    conv_vecs = [per_layer_row(a) for a in (cv_b_dw, cv_ln_g, cv_ln_b)]
    b_pw1, b_pw2 = per_layer_row(cv_b_pw1), per_layer_row(cv_b_pw2)

    xl, xc = x, ctx
    for layer in range(DEPTH):
        j = layer // 2
        even = layer % 2 == 0
        ctx_pre = layer <= LAST_CTX_READ
        ctx_full = layer < LAST_CTX_READ
        ffn = lambda xx, mrow, fi, sub: _ffn(xx, mods, layer, mrow, w13, w2, fi, ln_g, ln_b, sub)

        xl = ffn(xl, lat_row, 0, 0)
        if ctx_pre:
            xc = ffn(xc, ctx_row, 0, 0)

        if even:
            lam_init = 0.8 - 0.6 * math.exp(-0.3 * layer)
            lv = da_lambda[j].astype(F32)
            lam = (jnp.exp(jnp.sum(lv[0] * lv[1])) - jnp.exp(jnp.sum(lv[2] * lv[3])) + lam_init).reshape(1)
            lb_f = lower_bounds[j, :GROUP_W].reshape(N_HEADS, HEAD_W)
            lb_b = lower_bounds[j, GROUP_W:].reshape(N_HEADS, HEAD_W)
            pb_l, pf_l = _inproj(xl, mods, layer, lat_row, w_in, j, rope_tabs)
            pb_c, pf_c = _inproj(xc, mods, layer, ctx_row, w_in, j, None)
            a_l = _attention(lam, pb_l, [pb_l, pb_c])
            zeros = jnp.zeros((bsz, N_HEADS, HEAD_W, HEAD_W), F32)
            ocf, s_f = _hgrn_scan(pb_c, pf_c, lb_f, zeros, False, ctx_full)
            ocb, s_b = _hgrn_scan(pb_c, pf_c, lb_b, zeros, True, ctx_full)
            olf, _ = _hgrn_scan(pb_l, pf_l, lb_f, s_f, False, True)
            olb, _ = _hgrn_scan(pb_l, pf_l, lb_b, s_b, True, True)
            merge = lambda xx, mrow, a, o_f, o_b, pb: _merge(
                xx, mods, layer, mrow, a, o_f, o_b, pb, da_norm3, hg_norm3, w_out, j, ln_g, ln_b, 1.0 - lam_init)
            if ctx_full:
                a_c = _attention(lam, pb_c, [pb_c])
                xc = merge(xc, ctx_row, a_c, ocf, ocb, pb_c)
            xl = merge(xl, lat_row, a_l, olf, olb, pb_l)
        else:
            conv = lambda xx, mrow, u: _conv(xx, mods, layer, mrow, u, cv_w_dw, *conv_vecs, w_pw2, b_pw2, j,
                                             ln_g, ln_b)
            if ctx_full:
                xc = conv(xc, ctx_row, _glu(xc, mods, layer, ctx_row, w_pw1, b_pw1, j))
            xl = conv(xl, lat_row, _glu(xl, mods, layer, lat_row, w_pw1, b_pw1, j))

        xl = ffn(xl, lat_row, 1, 2)
        if ctx_full:
            xc = ffn(xc, ctx_row, 1, 2)
    return xl
```

```python
import functools
import math

import jax
import jax.numpy as jnp
from jax import lax
from jax.experimental import pallas as pl
from jax.experimental.pallas import tpu as pltpu

F32 = jnp.float32
BF16 = jnp.bfloat16

D_MODEL = 1024
DEPTH = 4
GRID_W = 64
N_HEADS = 4
HEAD_W = 128
QK_DIM = 64
GROUP_W = N_HEADS * HEAD_W
D_FF = 2816
N_MOD = 9
CONV_WIDTH = 31
CONV_HALO = 16
ROPE_BASE = 10000.0
LAST_CTX_READ = 2 * ((DEPTH - 1) // 2)
ALPHA = (2 * DEPTH) ** 0.25
EPS = 1e-5
HG_CHUNK = 128
FFN_CHUNK = D_FF // 2
ROW_TILE = 512
ATTN_SUB_ROWS = 256
CONV_ROWS = 64
VMEM_LIMIT = 56 * 1024 * 1024

PB_WIDTH = 6 * GROUP_W
PF_WIDTH = 2 * GROUP_W
PB_Q, PB_K, PB_V, PB_HQ, PB_HI, PB_HG = range(6)

NT_DIMS = (((1,), (1,)), ((), ()))
TN_DIMS = (((0,), (0,)), ((), ()))


def _params(*sem):
    return pltpu.CompilerParams(dimension_semantics=sem, vmem_limit_bytes=VMEM_LIMIT)


def _layer_norm(z, g, b):
    mu = jnp.mean(z, axis=-1, keepdims=True)
    zc = z - mu
    var = jnp.mean(zc * zc, axis=-1, keepdims=True)
    return zc * lax.rsqrt(var + EPS) * g + b


def _rms_norm(z, g):
    return z * lax.rsqrt(jnp.mean(z * z, axis=-1, keepdims=True) + EPS) * g


def _silu(a):
    return a * jax.nn.sigmoid(a)


def _row_tile(t):
    return min(t, ROW_TILE)


def _mod_spec(layer, mod_row, grid_rank):
    if grid_rank == 2:
        return pl.BlockSpec((None, None, N_MOD, D_MODEL), lambda bi, i: (layer, mod_row(bi), 0, 0))
    return pl.BlockSpec((None, None, N_MOD, D_MODEL), lambda bi, i, j: (layer, mod_row(bi), 0, 0))


def _mods_kernel(c_ref, w_ref, b_ref, o_ref):
    a = _silu(c_ref[...]).astype(BF16)
    o_ref[...] = jnp.dot(a, w_ref[...].astype(BF16), preferred_element_type=F32) + b_ref[...]


def _mods(cc, w_mod, b_mod):
    n = cc.shape[0]
    tn = 1024
    return pl.pallas_call(
        _mods_kernel,
        out_shape=jax.ShapeDtypeStruct((DEPTH, n, N_MOD * D_MODEL), F32),
        grid=(DEPTH, N_MOD * D_MODEL // tn),
        in_specs=[
            pl.BlockSpec((n, D_MODEL), lambda l, j: (0, 0)),
            pl.BlockSpec((None, D_MODEL, tn), lambda l, j: (l, 0, j)),
            pl.BlockSpec((None, 1, tn), lambda l, j: (l, 0, j)),
        ],
        out_specs=pl.BlockSpec((None, n, tn), lambda l, j: (l, 0, j)),
        compiler_params=_params("parallel", "parallel"),
        name="mods",
    )(cc, w_mod, b_mod.reshape(DEPTH, 1, N_MOD * D_MODEL))


def _ffn_kernel(xn_ref, xp_ref, modn_ref, modp_ref, w13_ref, w2_ref, g_ref, b_ref, o_ref, h_scr, acc_scr,
                *, sub, n_tiles):
    s = pl.program_id(0)

    def prologue(dst):
        shift = modn_ref[3 * sub:3 * sub + 1, :]
        scale = modn_ref[3 * sub + 1:3 * sub + 2, :]
        h_scr[dst] = (xn_ref[...] * (1.0 + scale) + shift).astype(BF16)

    def epilogue(src):
        gate = modp_ref[3 * sub + 2:3 * sub + 3, :]
        z = ALPHA * xp_ref[...] + gate * (0.5 * acc_scr[src])
        o_ref[...] = _layer_norm(z, g_ref[sub:sub + 1, :], b_ref[sub:sub + 1, :])

    def matmuls(slot):
        h = h_scr[slot]
        for fc in range(D_FF // FFN_CHUNK):
            lo = fc * FFN_CHUNK
            a = jnp.dot(h, w13_ref[:, lo:lo + FFN_CHUNK], preferred_element_type=F32)
            b = jnp.dot(h, w13_ref[:, D_FF + lo:D_FF + lo + FFN_CHUNK], preferred_element_type=F32)
            y = jnp.dot((_silu(a) * b).astype(BF16), w2_ref[lo:lo + FFN_CHUNK, :], preferred_element_type=F32)
            if fc == 0:
                acc_scr[slot] = y
            else:
                acc_scr[slot] += y

    @pl.when(s == 0)
    def _():
        prologue(0)
        acc_scr[1] = jnp.zeros(acc_scr.shape[1:], F32)

    for parity in range(2):
        @pl.when((s >= 1) & (s <= n_tiles) & (s % 2 == parity))
        def _():
            epilogue(parity)
            prologue(parity)
            matmuls(1 - parity)

    @pl.when(s == n_tiles + 1)
    def _():
        epilogue((n_tiles + 1) % 2)


def _ffn(x, mods, layer, mod_row, w13, w2, fi, ln_g, ln_b, sub):
    bsz, t, d = x.shape
    tm = _row_tile(t)
    per_b = t // tm
    n = bsz * per_b
    xf = x.reshape(bsz * t, d)
    nxt = lambda s: jnp.minimum(s, n - 1)
    prv = lambda s: jnp.clip(s - 2, 0, n - 1)
    resident = dict(pipeline_mode=pl.Buffered(1))
    out = pl.pallas_call(
        functools.partial(_ffn_kernel, sub=sub, n_tiles=n),
        out_shape=jax.ShapeDtypeStruct(xf.shape, F32),
        grid=(n + 2,),
        in_specs=[
            pl.BlockSpec((tm, d), lambda s: (nxt(s), 0)),
            pl.BlockSpec((tm, d), lambda s: (prv(s), 0)),
            pl.BlockSpec((None, None, N_MOD, d), lambda s: (layer, mod_row(nxt(s) // per_b), 0, 0)),
            pl.BlockSpec((None, None, N_MOD, d), lambda s: (layer, mod_row(prv(s) // per_b), 0, 0)),
            pl.BlockSpec((None, None, d, 2 * D_FF), lambda s: (layer, fi, 0, 0), **resident),
            pl.BlockSpec((None, None, D_FF, d), lambda s: (layer, fi, 0, 0), **resident),
            pl.BlockSpec((None, 3, d), lambda s: (layer, 0, 0)),
            pl.BlockSpec((None, 3, d), lambda s: (layer, 0, 0)),
        ],
        out_specs=pl.BlockSpec((tm, d), lambda s: (prv(s), 0)),
        scratch_shapes=[pltpu.VMEM((2, tm, d), BF16), pltpu.VMEM((2, tm, d), F32)],
        compiler_params=_params("arbitrary"),
        name="ffn",
    )(xf, xf, mods, mods, w13, w2, ln_g, ln_b)
    return out.reshape(x.shape)


def _inproj_kernel(*refs, rope):
    if rope:
        x_ref, mod_ref, w_ref, cos_ref, sa_ref, sb_ref, ob_ref, of_ref = refs
    else:
        x_ref, mod_ref, w_ref, ob_ref, of_ref = refs
    h = (x_ref[...] * (1.0 + mod_ref[4:5, :]) + mod_ref[3:4, :]).astype(BF16)
    w2 = 2 * GROUP_W
    proj = lambda grp: jnp.dot(h, w_ref[:, grp * w2:(grp + 1) * w2], preferred_element_type=F32)
    y = proj(0)
    if rope:
        wide = lambda r: jnp.concatenate([r[...]] * (w2 // HEAD_W), axis=1)
        y = (y * wide(cos_ref) + pltpu.roll(y, w2 - QK_DIM // 4, 1) * wide(sa_ref)
             + pltpu.roll(y, QK_DIM // 4, 1) * wide(sb_ref))
    ob_ref[:, 0:w2] = y.astype(BF16)
    ob_ref[:, w2:2 * w2] = proj(1).astype(BF16)
    of_ref[...] = proj(2)
    ob_ref[:, 2 * w2:3 * w2] = proj(3).astype(BF16)


def _inproj(x, mods, layer, mod_row, w_in, j, rope_tabs):
    bsz, t, d = x.shape
    tm = _row_tile(t)
    rope = rope_tabs is not None
    in_specs = [
        pl.BlockSpec((None, tm, d), lambda bi, i: (bi, i, 0)),
        _mod_spec(layer, mod_row, 2),
        pl.BlockSpec((None, d, 8 * GROUP_W), lambda bi, i: (j, 0, 0)),
    ]
    args = [x, mods, w_in]
    if rope:
        in_specs += [pl.BlockSpec((tm, HEAD_W), lambda bi, i: (i, 0))] * 3
        args += list(rope_tabs)
    return pl.pallas_call(
        functools.partial(_inproj_kernel, rope=rope),
        out_shape=[jax.ShapeDtypeStruct((bsz, t, PB_WIDTH), BF16), jax.ShapeDtypeStruct((bsz, t, PF_WIDTH), F32)],
        grid=(bsz, t // tm),
        in_specs=in_specs,
        out_specs=[pl.BlockSpec((None, tm, PB_WIDTH), lambda bi, i: (bi, i, 0)),
                   pl.BlockSpec((None, tm, PF_WIDTH), lambda bi, i: (bi, i, 0))],
        compiler_params=_params("parallel", "parallel"),
        name="inproj",
    )(*args)


def _rope_tables(n_tokens):
    pos = jnp.arange(n_tokens)
    half = QK_DIM // 2
    inv_freq = ROPE_BASE ** (-jnp.arange(0, half, 2, dtype=F32) / half)

    def table(p):
        ang = p.astype(F32)[:, None] * inv_freq[None, :]
        ang = jnp.concatenate([ang, ang], axis=-1)
        return jnp.cos(ang), jnp.sin(ang)

    (cos_r, sin_r), (cos_c, sin_c) = table(pos // GRID_W), table(pos % GRID_W)
    cos = jnp.concatenate([cos_r, cos_c, cos_r, cos_c], axis=-1)
    sin = jnp.concatenate([sin_r, sin_c, sin_r, sin_c], axis=-1)
    first = (jnp.arange(HEAD_W) % half) < half // 2
    return cos, jnp.where(first, -sin, 0.0), jnp.where(first, 0.0, sin)


def _attn_kernel(lam_ref, q_ref, *refs, n_src, n_sub):
    kv_refs, o_ref = refs[:2 * n_src], refs[2 * n_src]
    rows = q_ref.shape[0] // n_sub
    lane = lax.broadcasted_iota(jnp.int32, (1, HEAD_W), 1)
    ks = [kv_refs[2 * i][...] for i in range(n_src)]
    vs = [jnp.concatenate([kv_refs[2 * i + 1][...], jnp.ones(kv_refs[2 * i + 1].shape, BF16)], axis=1)
          for i in range(n_src)]
    units = [(sub, comp) for sub in range(n_sub) for comp in range(2)]

    def scores(unit):
        sub, comp = unit
        q = q_ref[sub * rows:(sub + 1) * rows, :] * (QK_DIM ** -0.5)
        sel = (lane < QK_DIM) if comp == 0 else (lane >= QK_DIM)
        qc = jnp.where(sel, q, jnp.zeros_like(q))
        return [lax.dot_general(qc, k, NT_DIMS, preferred_element_type=F32) for k in ks]

    def softmax_numerators(ss):
        m = functools.reduce(jnp.maximum, [jnp.max(s, axis=-1, keepdims=True) for s in ss])
        return [jnp.exp((s - m).astype(BF16)) for s in ss]

    def mix(ps):
        acc = functools.reduce(jnp.add, [jnp.dot(p, v, preferred_element_type=F32) for p, v in zip(ps, vs)])
        return acc[:, :HEAD_W] / acc[:, HEAD_W:HEAD_W + 1]

    n = len(units)
    ss, ps, mixed = {}, {}, {}
    for t in range(n + 2):
        if t < n:
            ss[t] = scores(units[t])
        if 0 <= t - 2 < n:
            mixed[units[t - 2]] = mix(ps.pop(t - 2))
        if 0 <= t - 1 < n:
            ps[t - 1] = softmax_numerators(ss.pop(t - 1))
    lam = lam_ref[0]
    for sub in range(n_sub):
        o_ref[sub * rows:(sub + 1) * rows, :] = mixed[(sub, 0)] - lam * mixed[(sub, 1)]


def _attention(lam, p_q, kv_sources):
    bsz, t, _ = p_q.shape
    tq = _row_tile(t)
    n_src = len(kv_sources)
    in_specs = [
        pl.BlockSpec(memory_space=pltpu.SMEM),
        pl.BlockSpec((None, tq, HEAD_W), lambda bi, h, i: (bi, i, PB_Q * N_HEADS + h)),
    ]
    args = [lam, p_q]
    for src in kv_sources:
        tk = src.shape[1]
        in_specs.append(pl.BlockSpec((None, tk, HEAD_W), lambda bi, h, i: (bi, 0, PB_K * N_HEADS + h)))
        in_specs.append(pl.BlockSpec((None, tk, HEAD_W), lambda bi, h, i: (bi, 0, PB_V * N_HEADS + h)))
        args += [src, src]
    return pl.pallas_call(
        functools.partial(_attn_kernel, n_src=n_src, n_sub=max(tq // ATTN_SUB_ROWS, 1)),
        out_shape=jax.ShapeDtypeStruct((bsz, t, GROUP_W), F32),
        grid=(bsz, N_HEADS, t // tq),
        in_specs=in_specs,
        out_specs=pl.BlockSpec((None, tq, HEAD_W), lambda bi, h, i: (bi, i, h)),
        compiler_params=_params("parallel", "parallel", "arbitrary"),
        name="diff_attn",
    )(*args)


def _time_cumsum(x, tri):
    hi = x.astype(BF16)
    r1 = x - hi.astype(F32)
    mid = r1.astype(BF16)
    lo = (r1 - mid.astype(F32)).astype(BF16)
    parts = [jnp.dot(tri, part, preferred_element_type=F32) for part in (hi, mid, lo)]
    return parts[0] + (parts[1] + parts[2])


def _block_row(g, blk, idx):
    n, k = g.shape
    if blk >= 8:
        g4 = g.reshape(n // blk, blk // 8, 8, k)
        ref = g4[:, idx // 8:idx // 8 + 1, idx % 8:idx % 8 + 1, :]
        return jnp.broadcast_to(ref, g4.shape).reshape(n, k)
    g3 = g.reshape(n // 8, 8, k)
    sub = lax.broadcasted_iota(jnp.int32, (1, 8, 1), 1) // blk
    out = None
    for j in range(8 // blk):
        ref = jnp.broadcast_to(g3[:, j * blk + idx:j * blk + idx + 1, :], g3.shape)
        out = ref if out is None else jnp.where(sub == j, ref, out)
    return out.reshape(n, k)


def _hgrn_kernel(*refs, reverse, with_output):
    if with_output:
        q_ref, f_ref, i_ref, lb_ref, s0_ref, o_ref, s_ref = refs
    else:
        q_ref, f_ref, i_ref, lb_ref, s0_ref, s_ref = refs
    n = f_ref.shape[0]

    @pl.when(pl.program_id(1) == 0)
    def _():
        s_ref[...] = s0_ref[...]

    last = 0 if reverse else n - 1
    n_levels = n.bit_length() - 1
    rt = lax.broadcasted_iota(jnp.int32, (n, n), 0)
    cs = lax.broadcasted_iota(jnp.int32, (n, n), 1)
    tri = ((cs >= rt) if reverse else (cs <= rt)).astype(BF16)
    if with_output:
        differ = rt ^ cs
        top_bit = functools.reduce(jnp.add, [(differ >= (1 << b)).astype(jnp.int32) for b in range(1, n_levels)])
        earlier = (cs > rt) if reverse else (cs < rt)
        level = jnp.where(earlier, top_bit + 1, jnp.where(rt == cs, 0, -1))
    heads = range(N_HEADS)
    lanes = [slice(h * HEAD_W, (h + 1) * HEAD_W) for h in heads]
    kk, g = [], []
    for h in heads:
        lb = lb_ref[h:h + 1, :]
        f = lb + (1.0 - lb) * jax.nn.sigmoid(f_ref[:, lanes[h]])
        kk.append(1.0 - f)
        g.append(_time_cumsum(jnp.log2(f), tri))
    if with_output:
        qq = [_silu(q_ref[:, lanes[h]].astype(F32)) for h in heads]
        qb = [qq[h].astype(BF16) for h in heads]
        kb = [kk[h].astype(BF16) for h in heads]
        att = [jnp.where(level == 0, lax.dot_general(qb[h], kb[h], NT_DIMS, preferred_element_type=F32), 0.0)
               for h in heads]
        for lv in range(1, n_levels + 1):
            half = 1 << (lv - 1)
            for h in heads:
                boundary = _block_row(g[h], 2 * half, half if reverse else half - 1)
                e = jnp.exp2(-jnp.abs(g[h] - boundary)).astype(BF16)
                p = lax.dot_general(qb[h] * e, kb[h] * e, NT_DIMS, preferred_element_type=F32)
                att[h] = jnp.where(level == lv, p, att[h])
        for h in heads:
            o = jnp.dot(att[h].astype(BF16), i_ref[:, lanes[h]], preferred_element_type=F32)
            o = o + lax.dot_general((qq[h] * jnp.exp2(g[h])).astype(BF16), s_ref[h].astype(BF16), NT_DIMS,
                                    preferred_element_type=F32)
            o_ref[:, lanes[h]] = o
    for h in heads:
        g_last = g[h][last:last + 1, :]
        kx = (kk[h] * jnp.exp2(g_last - g[h])).astype(BF16)
        s_ref[h] = jnp.exp2(g_last) * s_ref[h] + lax.dot_general(i_ref[:, lanes[h]], kx, TN_DIMS,
                                                                 preferred_element_type=F32)


def _hgrn_scan(p_b, p_f, lb, s0, reverse, with_output):
    bsz, t, _ = p_b.shape
    c = HG_CHUNK
    nc = t // c
    cmap = (lambda ci: nc - 1 - ci) if reverse else (lambda ci: ci)
    col = lambda blk: (lambda bi, ci: (bi, cmap(ci), blk))
    state_spec = pl.BlockSpec((None, N_HEADS, HEAD_W, HEAD_W), lambda bi, ci: (bi, 0, 0, 0))
    out_shape = [jax.ShapeDtypeStruct((bsz, N_HEADS, HEAD_W, HEAD_W), F32)]
    out_specs = [state_spec]
    if with_output:
        out_shape.insert(0, jax.ShapeDtypeStruct((bsz, t, GROUP_W), F32))
        out_specs.insert(0, pl.BlockSpec((None, c, GROUP_W), col(0)))
    res = pl.pallas_call(
        functools.partial(_hgrn_kernel, reverse=reverse, with_output=with_output),
        out_shape=out_shape,
        grid=(bsz, nc),
        in_specs=[
            pl.BlockSpec((None, c, GROUP_W), col(PB_HQ)),
            pl.BlockSpec((None, c, GROUP_W), col(1 if reverse else 0)),
            pl.BlockSpec((None, c, GROUP_W), col(PB_HI)),
            pl.BlockSpec((N_HEADS, HEAD_W), lambda bi, ci: (0, 0)),
            state_spec,
        ],
        out_specs=out_specs,
        compiler_params=_params("parallel", "arbitrary"),
        name="hgrn_bwd" if reverse else "hgrn_fwd",
    )(p_b, p_f, p_b, lb, s0)
    return (res[0], res[1]) if with_output else (None, res[0])


def _merge_kernel(x_ref, mod_ref, a_ref, of_ref, ob_ref, gt_ref, dn_ref, hn_ref, w_ref, g_ref, b_ref, o_ref,
                  *, da_scale):
    a = a_ref[...]
    o = of_ref[...] + ob_ref[...]
    parts_a, parts_o = [], []
    for h in range(N_HEADS):
        sl = slice(h * HEAD_W, (h + 1) * HEAD_W)
        parts_a.append(_rms_norm(a[:, sl], dn_ref[:, sl]) * da_scale)
        parts_o.append(_rms_norm(o[:, sl], hn_ref[:, sl]))
    ya = jnp.concatenate(parts_a, axis=1)
    yo = jnp.concatenate(parts_o, axis=1) * _silu(gt_ref[...].astype(F32))
    y = jnp.dot(jnp.concatenate([ya, yo], axis=1).astype(BF16), w_ref[...], preferred_element_type=F32)
    z = ALPHA * x_ref[...] + mod_ref[5:6, :] * y
    o_ref[...] = _layer_norm(z, g_ref[1:2, :], b_ref[1:2, :])


def _merge(x, mods, layer, mod_row, attn, o_f, o_b, p_b, da_norm, hg_norm, w_out, j, ln_g, ln_b, da_scale):
    bsz, t, d = x.shape
    tm = _row_tile(t)
    grp = pl.BlockSpec((None, tm, GROUP_W), lambda bi, i: (bi, i, 0))
    vec = lambda n: pl.BlockSpec((None, 1, n), lambda bi, i: (j, 0, 0))
    ln = pl.BlockSpec((None, 3, d), lambda bi, i: (layer, 0, 0))
    return pl.pallas_call(
        functools.partial(_merge_kernel, da_scale=da_scale),
        out_shape=jax.ShapeDtypeStruct(x.shape, F32),
        grid=(bsz, t // tm),
        in_specs=[
            pl.BlockSpec((None, tm, d), lambda bi, i: (bi, i, 0)),
            _mod_spec(layer, mod_row, 2),
            grp, grp, grp,
            pl.BlockSpec((None, tm, GROUP_W), lambda bi, i: (bi, i, PB_HG)),
            vec(GROUP_W), vec(GROUP_W),
            pl.BlockSpec((None, d, d), lambda bi, i: (j, 0, 0)),
            ln, ln,
        ],
        out_specs=pl.BlockSpec((None, tm, d), lambda bi, i: (bi, i, 0)),
        compiler_params=_params("parallel", "parallel"),
        name="even_merge",
    )(x, mods, attn, o_f, o_b, p_b, da_norm, hg_norm, w_out, ln_g, ln_b)


def _glu_kernel(x_ref, mod_ref, w_ref, b_ref, o_ref):
    h = (x_ref[...] * (1.0 + mod_ref[4:5, :]) + mod_ref[3:4, :]).astype(BF16)
    y = jnp.dot(h, w_ref[...], preferred_element_type=F32) + b_ref[...]
    d = o_ref.shape[-1]
    o_ref[...] = y[:, :d] * jax.nn.sigmoid(y[:, d:])


def _glu(x, mods, layer, mod_row, w_pw1, b_pw1, j):
    bsz, t, d = x.shape
    tm = _row_tile(t)
    return pl.pallas_call(
        _glu_kernel,
        out_shape=jax.ShapeDtypeStruct(x.shape, F32),
        grid=(bsz, t // tm),
        in_specs=[
            pl.BlockSpec((None, tm, d), lambda bi, i: (bi, i, 0)),
            _mod_spec(layer, mod_row, 2),
            pl.BlockSpec((None, d, 2 * d), lambda bi, i: (j, 0, 0)),
            pl.BlockSpec((None, 1, 2 * d), lambda bi, i: (j, 0, 0)),
        ],
        out_specs=pl.BlockSpec((None, tm, d), lambda bi, i: (bi, i, 0)),
        compiler_params=_params("parallel", "parallel"),
        name="conv_glu",
    )(x, mods, w_pw1, b_pw1)


def _conv_kernel(x_ref, mod_ref, u_ref, up_ref, un_ref, wdw_ref, bdw_ref, cg_ref, cb_ref, w_ref, b2_ref,
                 g_ref, b_ref, o_ref, ext_ref, sh_ref, acc_ref):
    i = pl.program_id(1)
    tm = u_ref.shape[0]
    ext_ref[0:CONV_HALO, :] = jnp.where(i > 0, up_ref[...], 0.0)
    ext_ref[CONV_HALO:CONV_HALO + tm, :] = u_ref[...]
    ext_ref[CONV_HALO + tm:, :] = jnp.where(i < pl.num_programs(1) - 1, un_ref[...], 0.0)
    off = CONV_HALO - CONV_WIDTH // 2
    span = sh_ref.shape[1]
    for cb in range(u_ref.shape[1] // HEAD_W):
        cols = slice(cb * HEAD_W, (cb + 1) * HEAD_W)
        for p in range(1, 8):
            sh_ref[p - 1] = ext_ref[p:p + span, cols]

        def row_block(rb, carry):
            r0 = pl.multiple_of(rb * CONV_ROWS, CONV_ROWS)
            acc = None
            for w in range(CONV_WIDTH):
                p, a = (off + w) % 8, (off + w) // 8
                rows = pl.ds(r0 + 8 * a, CONV_ROWS)
                win = ext_ref[rows, cols] if p == 0 else sh_ref[p - 1, rows, :]
                term = win * wdw_ref[w:w + 1, cols]
                acc = term if acc is None else acc + term
            acc_ref[pl.ds(r0, CONV_ROWS), cols] = acc
            return carry

        lax.fori_loop(0, tm // CONV_ROWS, row_block, 0)
    u = _silu(_layer_norm(acc_ref[...] + bdw_ref[...], cg_ref[...], cb_ref[...]))
    y = jnp.dot(u.astype(BF16), w_ref[...], preferred_element_type=F32) + b2_ref[...]
    z = ALPHA * x_ref[...] + mod_ref[5:6, :] * y
    o_ref[...] = _layer_norm(z, g_ref[1:2, :], b_ref[1:2, :])


def _conv(x, mods, layer, mod_row, u, w_dw, b_dw, cg, cb, w_pw2, b_pw2, j, ln_g, ln_b):
    bsz, t, d = x.shape
    tm = _row_tile(t)
    r = tm // CONV_HALO
    n_halo = t // CONV_HALO
    row = pl.BlockSpec((None, tm, d), lambda bi, i: (bi, i, 0))
    vec = pl.BlockSpec((None, 1, d), lambda bi, i: (j, 0, 0))
    ln = pl.BlockSpec((None, 3, d), lambda bi, i: (layer, 0, 0))
    return pl.pallas_call(
        _conv_kernel,
        out_shape=jax.ShapeDtypeStruct(x.shape, F32),
        grid=(bsz, t // tm),
        in_specs=[
            row,
            _mod_spec(layer, mod_row, 2),
            row,
            pl.BlockSpec((None, CONV_HALO, d), lambda bi, i: (bi, jnp.maximum(i * r - 1, 0), 0)),
            pl.BlockSpec((None, CONV_HALO, d), lambda bi, i: (bi, jnp.minimum((i + 1) * r, n_halo - 1), 0)),
            pl.BlockSpec((None, CONV_WIDTH, d), lambda bi, i: (j, 0, 0)),
            vec, vec, vec,
            pl.BlockSpec((None, d, d), lambda bi, i: (j, 0, 0)),
            vec, ln, ln,
        ],
        out_specs=row,
        scratch_shapes=[pltpu.VMEM((tm + 2 * CONV_HALO, d), F32),
                        pltpu.VMEM((7, tm + 8 * ((CONV_WIDTH + 7) // 8 - 1), HEAD_W), F32),
                        pltpu.VMEM((tm, d), F32)],
        compiler_params=_params("parallel", "parallel"),
        name="conv_mix",
    )(x, mods, u, u, u, w_dw, b_dw, cg, cb, w_pw2, b_pw2, ln_g, ln_b)


def kernel(x, c, ctx, c_ctx, w_mod, b_mod, ln_g, ln_b, ffn_w13, ffn_w2, ev_w_in, ev_w_out, da_lambda, da_norm,
           hg_lb, hg_norm, cv_w_pw1, cv_b_pw1, cv_w_dw, cv_b_dw, cv_ln_g, cv_ln_b, cv_w_pw2, cv_b_pw2):
    bsz, seq, d = x.shape
    per_layer_row = lambda a: a.reshape(a.shape[0], 1, a.shape[1])

    cc = jnp.zeros((16, d), F32).at[:bsz].set(c).at[bsz].set(c_ctx)
    mods = _mods(cc, w_mod, b_mod).reshape(DEPTH, 16, N_MOD, d)
    lat_row = lambda bi: bi
    ctx_row = lambda bi: bsz
    rope_tabs = _rope_tables(seq)
    lb_w = jax.nn.softmax(hg_lb.astype(F32), axis=0)
    lower_bounds = jnp.cumsum(lb_w, axis=0) - lb_w[:1]

    w13, w2 = ffn_w13.astype(BF16), ffn_w2.astype(BF16)
    w_in, w_out = ev_w_in.astype(BF16), ev_w_out.astype(BF16)
    w_pw1, w_pw2 = cv_w_pw1.astype(BF16), cv_w_pw2.astype(BF16)
    da_norm3, hg_norm3 = per_layer_row(da_norm), per_layer_row(hg_norm)
    conv_vecs = [per_layer_row(a) for a in (cv_b_dw, cv_ln_g, cv_ln_b)]
    b_pw1, b_pw2 = per_layer_row(cv_b_pw1), per_layer_row(cv_b_pw2)

    xl, xc = x, ctx
    for layer in range(DEPTH):
        j = layer // 2
        even = layer % 2 == 0
        ctx_pre = layer <= LAST_CTX_READ
        ctx_full = layer < LAST_CTX_READ
        ffn = lambda xx, mrow, fi, sub: _ffn(xx, mods, layer, mrow, w13, w2, fi, ln_g, ln_b, sub)

        xl = ffn(xl, lat_row, 0, 0)
        if ctx_pre:
            xc = ffn(xc, ctx_row, 0, 0)

        if even:
            lam_init = 0.8 - 0.6 * math.exp(-0.3 * layer)
            lv = da_lambda[j].astype(F32)
            lam = (jnp.exp(jnp.sum(lv[0] * lv[1])) - jnp.exp(jnp.sum(lv[2] * lv[3])) + lam_init).reshape(1)
            lb_f = lower_bounds[j, :GROUP_W].reshape(N_HEADS, HEAD_W)
            lb_b = lower_bounds[j, GROUP_W:].reshape(N_HEADS, HEAD_W)
            pb_l, pf_l = _inproj(xl, mods, layer, lat_row, w_in, j, rope_tabs)
            pb_c, pf_c = _inproj(xc, mods, layer, ctx_row, w_in, j, None)
            a_l = _attention(lam, pb_l, [pb_l, pb_c])
            zeros = jnp.zeros((bsz, N_HEADS, HEAD_W, HEAD_W), F32)
            ocf, s_f = _hgrn_scan(pb_c, pf_c, lb_f, zeros, False, ctx_full)
            ocb, s_b = _hgrn_scan(pb_c, pf_c, lb_b, zeros, True, ctx_full)
            olf, _ = _hgrn_scan(pb_l, pf_l, lb_f, s_f, False, True)
            olb, _ = _hgrn_scan(pb_l, pf_l, lb_b, s_b, True, True)
            merge = lambda xx, mrow, a, o_f, o_b, pb: _merge(
                xx, mods, layer, mrow, a, o_f, o_b, pb, da_norm3, hg_norm3, w_out, j, ln_g, ln_b, 1.0 - lam_init)
            if ctx_full:
                a_c = _attention(lam, pb_c, [pb_c])
                xc = merge(xc, ctx_row, a_c, ocf, ocb, pb_c)
            xl = merge(xl, lat_row, a_l, olf, olb, pb_l)
        else:
            conv = lambda xx, mrow, u: _conv(xx, mods, layer, mrow, u, cv_w_dw, *conv_vecs, w_pw2, b_pw2, j,
                                             ln_g, ln_b)
            if ctx_full:
                xc = conv(xc, ctx_row, _glu(xc, mods, layer, ctx_row, w_pw1, b_pw1, j))
            xl = conv(xl, lat_row, _glu(xl, mods, layer, lat_row, w_pw1, b_pw1, j))

        xl = ffn(xl, lat_row, 1, 2)
        if ctx_full:
            xc = ffn(xc, ctx_row, 1, 2)
    return xl
```

```python
import functools
import math

import jax
import jax.numpy as jnp
from jax import lax
from jax.experimental import pallas as pl
from jax.experimental.pallas import tpu as pltpu

F32 = jnp.float32
BF16 = jnp.bfloat16

D_MODEL = 1024
DEPTH = 4
GRID_W = 64
N_HEADS = 4
HEAD_W = 128
QK_DIM = 64
GROUP_W = N_HEADS * HEAD_W
D_FF = 2816
N_MOD = 9
CONV_WIDTH = 31
CONV_HALO = 16
ROPE_BASE = 10000.0
LAST_CTX_READ = 2 * ((DEPTH - 1) // 2)
ALPHA = (2 * DEPTH) ** 0.25
EPS = 1e-5
HG_CHUNK = 128
MXU_WIDTH = 256
FFN_SPLIT = (D_FF // MXU_WIDTH + 1) // 2 * MXU_WIDTH
FFN_CHUNKS = ((0, FFN_SPLIT), (FFN_SPLIT, D_FF))
ROW_TILE = 512
ATTN_ROW_TILE = 1024
ATTN_SUB_ROWS = 256
CONV_ROWS = 64
VMEM_LIMIT = 56 * 1024 * 1024

PB_WIDTH = 6 * GROUP_W
PF_WIDTH = 2 * GROUP_W
PB_Q, PB_K, PB_V, PB_HQ, PB_HI, PB_HG = range(6)

NT_DIMS = (((1,), (1,)), ((), ()))
TN_DIMS = (((0,), (0,)), ((), ()))


def _params(*sem):
    return pltpu.CompilerParams(dimension_semantics=sem, vmem_limit_bytes=VMEM_LIMIT)


def _layer_norm(z, g, b):
    mu = jnp.mean(z, axis=-1, keepdims=True)
    zc = z - mu
    var = jnp.mean(zc * zc, axis=-1, keepdims=True)
    return zc * lax.rsqrt(var + EPS) * g + b


def _rms_norm(z, g):
    return z * lax.rsqrt(jnp.mean(z * z, axis=-1, keepdims=True) + EPS) * g


def _silu(a):
    return a * jax.nn.sigmoid(a)


def _row_tile(t):
    return min(t, ROW_TILE)


def _mod_spec(layer, mod_row, grid_rank):
    if grid_rank == 2:
        return pl.BlockSpec((None, None, N_MOD, D_MODEL), lambda bi, i: (layer, mod_row(bi), 0, 0))
    return pl.BlockSpec((None, None, N_MOD, D_MODEL), lambda bi, i, j: (layer, mod_row(bi), 0, 0))


def _mods_kernel(c_ref, w_ref, b_ref, o_ref):
    a = _silu(c_ref[...]).astype(BF16)
    o_ref[...] = jnp.dot(a, w_ref[...].astype(BF16), preferred_element_type=F32) + b_ref[...]


def _mods(cc, w_mod, b_mod):
    n = cc.shape[0]
    tn = 1024
    return pl.pallas_call(
        _mods_kernel,
        out_shape=jax.ShapeDtypeStruct((DEPTH, n, N_MOD * D_MODEL), F32),
        grid=(DEPTH, N_MOD * D_MODEL // tn),
        in_specs=[
            pl.BlockSpec((n, D_MODEL), lambda l, j: (0, 0)),
            pl.BlockSpec((None, D_MODEL, tn), lambda l, j: (l, 0, j)),
            pl.BlockSpec((None, 1, tn), lambda l, j: (l, 0, j)),
        ],
        out_specs=pl.BlockSpec((None, n, tn), lambda l, j: (l, 0, j)),
        compiler_params=_params("parallel", "parallel"),
        name="mods",
    )(cc, w_mod, b_mod.reshape(DEPTH, 1, N_MOD * D_MODEL))


def _ffn_kernel(xn_ref, xp_ref, modn_ref, modp_ref, w13_ref, w2_ref, g_ref, b_ref, o_ref, h_scr, acc_scr,
                *, sub, n_tiles):
    s = pl.program_id(0)

    def prologue(dst):
        shift = modn_ref[3 * sub:3 * sub + 1, :]
        scale = modn_ref[3 * sub + 1:3 * sub + 2, :]
        h_scr[dst] = (xn_ref[...] * (1.0 + scale) + shift).astype(BF16)

    def epilogue(src):
        gate = modp_ref[3 * sub + 2:3 * sub + 3, :]
        z = ALPHA * xp_ref[...] + gate * (0.5 * acc_scr[src])
        o_ref[...] = _layer_norm(z, g_ref[sub:sub + 1, :], b_ref[sub:sub + 1, :])

    def matmuls(slot):
        h = h_scr[slot]
        for fc, (lo, hi) in enumerate(FFN_CHUNKS):
            a = jnp.dot(h, w13_ref[:, lo:hi], preferred_element_type=F32)
            b = jnp.dot(h, w13_ref[:, D_FF + lo:D_FF + hi], preferred_element_type=F32)
            y = jnp.dot((_silu(a) * b).astype(BF16), w2_ref[lo:hi, :], preferred_element_type=F32)
            if fc == 0:
                acc_scr[slot] = y
            else:
                acc_scr[slot] += y

    @pl.when(s == 0)
    def _():
        prologue(0)
        acc_scr[1] = jnp.zeros(acc_scr.shape[1:], F32)

    for parity in range(2):
        @pl.when((s >= 1) & (s <= n_tiles) & (s % 2 == parity))
        def _():
            epilogue(parity)
            prologue(parity)
            matmuls(1 - parity)

    @pl.when(s == n_tiles + 1)
    def _():
        epilogue((n_tiles + 1) % 2)


def _ffn(x, mods, layer, mod_row, w13, w2, fi, ln_g, ln_b, sub):
    bsz, t, d = x.shape
    tm = _row_tile(t)
    per_b = t // tm
    n = bsz * per_b
    xf = x.reshape(bsz * t, d)
    nxt = lambda s: jnp.minimum(s, n - 1)
    prv = lambda s: jnp.clip(s - 2, 0, n - 1)
    resident = dict(pipeline_mode=pl.Buffered(1))
    out = pl.pallas_call(
        functools.partial(_ffn_kernel, sub=sub, n_tiles=n),
        out_shape=jax.ShapeDtypeStruct(xf.shape, F32),
        grid=(n + 2,),
        in_specs=[
            pl.BlockSpec((tm, d), lambda s: (nxt(s), 0)),
            pl.BlockSpec((tm, d), lambda s: (prv(s), 0)),
            pl.BlockSpec((None, None, N_MOD, d), lambda s: (layer, mod_row(nxt(s) // per_b), 0, 0)),
            pl.BlockSpec((None, None, N_MOD, d), lambda s: (layer, mod_row(prv(s) // per_b), 0, 0)),
            pl.BlockSpec((None, None, d, 2 * D_FF), lambda s: (layer, fi, 0, 0), **resident),
            pl.BlockSpec((None, None, D_FF, d), lambda s: (layer, fi, 0, 0), **resident),
            pl.BlockSpec((None, 3, d), lambda s: (layer, 0, 0)),
            pl.BlockSpec((None, 3, d), lambda s: (layer, 0, 0)),
        ],
        out_specs=pl.BlockSpec((tm, d), lambda s: (prv(s), 0)),
        scratch_shapes=[pltpu.VMEM((2, tm, d), BF16), pltpu.VMEM((2, tm, d), F32)],
        compiler_params=_params("arbitrary"),
        name="ffn",
    )(xf, xf, mods, mods, w13, w2, ln_g, ln_b)
    return out.reshape(x.shape)


def _inproj_kernel(*refs, rope):
    if rope:
        x_ref, mod_ref, w_ref, cos_ref, sa_ref, sb_ref, ob_ref, of_ref = refs
    else:
        x_ref, mod_ref, w_ref, ob_ref, of_ref = refs
    h = (x_ref[...] * (1.0 + mod_ref[4:5, :]) + mod_ref[3:4, :]).astype(BF16)
    w2 = 2 * GROUP_W
    proj = lambda grp: jnp.dot(h, w_ref[:, grp * w2:(grp + 1) * w2], preferred_element_type=F32)
    y = proj(0)
    if rope:
        wide = lambda r: jnp.concatenate([r[...]] * (w2 // HEAD_W), axis=1)
        y = (y * wide(cos_ref) + pltpu.roll(y, w2 - QK_DIM // 4, 1) * wide(sa_ref)
             + pltpu.roll(y, QK_DIM // 4, 1) * wide(sb_ref))
    ob_ref[:, 0:w2] = y.astype(BF16)
    ob_ref[:, w2:2 * w2] = proj(1).astype(BF16)
    of_ref[...] = proj(2)
    ob_ref[:, 2 * w2:3 * w2] = proj(3).astype(BF16)


def _inproj(x, mods, layer, mod_row, w_in, j, rope_tabs):
    bsz, t, d = x.shape
    tm = _row_tile(t)
    rope = rope_tabs is not None
    in_specs = [
        pl.BlockSpec((None, tm, d), lambda bi, i: (bi, i, 0)),
        _mod_spec(layer, mod_row, 2),
        pl.BlockSpec((None, d, 8 * GROUP_W), lambda bi, i: (j, 0, 0)),
    ]
    args = [x, mods, w_in]
    if rope:
        in_specs += [pl.BlockSpec((tm, HEAD_W), lambda bi, i: (i, 0))] * 3
        args += list(rope_tabs)
    return pl.pallas_call(
        functools.partial(_inproj_kernel, rope=rope),
        out_shape=[jax.ShapeDtypeStruct((bsz, t, PB_WIDTH), BF16), jax.ShapeDtypeStruct((bsz, t, PF_WIDTH), F32)],
        grid=(bsz, t // tm),
        in_specs=in_specs,
        out_specs=[pl.BlockSpec((None, tm, PB_WIDTH), lambda bi, i: (bi, i, 0)),
                   pl.BlockSpec((None, tm, PF_WIDTH), lambda bi, i: (bi, i, 0))],
        compiler_params=_params("parallel", "parallel"),
        name="inproj",
    )(*args)


def _rope_tables(n_tokens):
    pos = jnp.arange(n_tokens)
    half = QK_DIM // 2
    inv_freq = ROPE_BASE ** (-jnp.arange(0, half, 2, dtype=F32) / half)

    def table(p):
        ang = p.astype(F32)[:, None] * inv_freq[None, :]
        ang = jnp.concatenate([ang, ang], axis=-1)
        return jnp.cos(ang), jnp.sin(ang)

    (cos_r, sin_r), (cos_c, sin_c) = table(pos // GRID_W), table(pos % GRID_W)
    cos = jnp.concatenate([cos_r, cos_c, cos_r, cos_c], axis=-1)
    sin = jnp.concatenate([sin_r, sin_c, sin_r, sin_c], axis=-1)
    first = (jnp.arange(HEAD_W) % half) < half // 2
    return cos, jnp.where(first, -sin, 0.0), jnp.where(first, 0.0, sin)


def _attn_kernel(lam_ref, q_ref, *refs, n_src, n_sub):
    kv_refs, o_ref = refs[:2 * n_src], refs[2 * n_src]
    rows = q_ref.shape[0] // n_sub
    lane = lax.broadcasted_iota(jnp.int32, (1, HEAD_W), 1)
    ks = [kv_refs[2 * i][...] for i in range(n_src)]
    vs = [jnp.concatenate([kv_refs[2 * i + 1][...], jnp.ones(kv_refs[2 * i + 1].shape, BF16)], axis=1)
          for i in range(n_src)]
    units = [(sub, comp) for sub in range(n_sub) for comp in range(2)]

    def scores(unit):
        sub, comp = unit
        q = q_ref[sub * rows:(sub + 1) * rows, :] * (QK_DIM ** -0.5)
        sel = (lane < QK_DIM) if comp == 0 else (lane >= QK_DIM)
        qc = jnp.where(sel, q, jnp.zeros_like(q))
        return [lax.dot_general(qc, k, NT_DIMS, preferred_element_type=F32) for k in ks]

    def softmax_numerators(ss):
        m = functools.reduce(jnp.maximum, [jnp.max(s, axis=-1, keepdims=True) for s in ss])
        return [jnp.exp((s - m).astype(BF16)) for s in ss]

    def mix(ps):
        acc = functools.reduce(jnp.add, [jnp.dot(p, v, preferred_element_type=F32) for p, v in zip(ps, vs)])
        return acc[:, :HEAD_W] / acc[:, HEAD_W:HEAD_W + 1]

    n = len(units)
    ss, ps, mixed = {}, {}, {}
    for t in range(n + 2):
        if t < n:
            ss[t] = scores(units[t])
        if 0 <= t - 2 < n:
            mixed[units[t - 2]] = mix(ps.pop(t - 2))
        if 0 <= t - 1 < n:
            ps[t - 1] = softmax_numerators(ss.pop(t - 1))
    lam = lam_ref[0]
    for sub in range(n_sub):
        o_ref[sub * rows:(sub + 1) * rows, :] = mixed[(sub, 0)] - lam * mixed[(sub, 1)]


def _attention(lam, p_q, kv_sources):
    bsz, t, _ = p_q.shape
    tq = min(t, ATTN_ROW_TILE)
    n_src = len(kv_sources)
    in_specs = [
        pl.BlockSpec(memory_space=pltpu.SMEM),
        pl.BlockSpec((None, tq, HEAD_W), lambda bi, h, i: (bi, i, PB_Q * N_HEADS + h)),
    ]
    args = [lam, p_q]
    for src in kv_sources:
        tk = src.shape[1]
        in_specs.append(pl.BlockSpec((None, tk, HEAD_W), lambda bi, h, i: (bi, 0, PB_K * N_HEADS + h)))
        in_specs.append(pl.BlockSpec((None, tk, HEAD_W), lambda bi, h, i: (bi, 0, PB_V * N_HEADS + h)))
        args += [src, src]
    return pl.pallas_call(
        functools.partial(_attn_kernel, n_src=n_src, n_sub=max(tq // ATTN_SUB_ROWS, 1)),
        out_shape=jax.ShapeDtypeStruct((bsz, t, GROUP_W), F32),
        grid=(bsz, N_HEADS, t // tq),
        in_specs=in_specs,
        out_specs=pl.BlockSpec((None, tq, HEAD_W), lambda bi, h, i: (bi, i, h)),
        compiler_params=_params("parallel", "parallel", "arbitrary"),
        name="diff_attn",
    )(*args)


def _time_cumsum(x, tri):
    hi = x.astype(BF16)
    r1 = x - hi.astype(F32)
    mid = r1.astype(BF16)
    lo = (r1 - mid.astype(F32)).astype(BF16)
    parts = [jnp.dot(tri, part, preferred_element_type=F32) for part in (hi, mid, lo)]
    return parts[0] + (parts[1] + parts[2])


def _block_row(g, blk, idx):
    n, k = g.shape
    if blk >= 8:
        g4 = g.reshape(n // blk, blk // 8, 8, k)
        ref = g4[:, idx // 8:idx // 8 + 1, idx % 8:idx % 8 + 1, :]
        return jnp.broadcast_to(ref, g4.shape).reshape(n, k)
    g3 = g.reshape(n // 8, 8, k)
    sub = lax.broadcasted_iota(jnp.int32, (1, 8, 1), 1) // blk
    out = None
    for j in range(8 // blk):
        ref = jnp.broadcast_to(g3[:, j * blk + idx:j * blk + idx + 1, :], g3.shape)
        out = ref if out is None else jnp.where(sub == j, ref, out)
    return out.reshape(n, k)


def _hgrn_kernel(*refs, with_output):
    ins = [refs[0:5], refs[5:10]]
    outs = refs[10:]
    o_refs, s_refs = (outs[0:2], outs[2:4]) if with_output else (None, outs[0:2])
    n = ins[0][1].shape[0]

    @pl.when(pl.program_id(1) == 0)
    def _():
        for d in range(2):
            s_refs[d][...] = ins[d][4][...]

    n_levels = n.bit_length() - 1
    rt = lax.broadcasted_iota(jnp.int32, (n, n), 0)
    cs = lax.broadcasted_iota(jnp.int32, (n, n), 1)
    tri = [(cs <= rt).astype(BF16), (cs >= rt).astype(BF16)]
    last = [n - 1, 0]
    if with_output:
        differ = rt ^ cs
        top_bit = functools.reduce(jnp.add, [(differ >= (1 << b)).astype(jnp.int32) for b in range(1, n_levels)])
        level = [jnp.where(earlier, top_bit + 1, jnp.where(rt == cs, 0, -1)) for earlier in (cs < rt, cs > rt)]
    chains = [(d, h) for d in range(2) for h in range(N_HEADS)]
    lanes = [slice(h * HEAD_W, (h + 1) * HEAD_W) for h in range(N_HEADS)]
    kk, g = {}, {}
    for d, h in chains:
        lb = ins[d][3][h:h + 1, :]
        f = lb + (1.0 - lb) * jax.nn.sigmoid(ins[d][1][:, lanes[h]])
        kk[d, h] = 1.0 - f
        g[d, h] = _time_cumsum(jnp.log2(f), tri[d])
    if with_output:
        qq = {(d, h): _silu(ins[d][0][:, lanes[h]].astype(F32)) for d, h in chains}
        qb = {c: qq[c].astype(BF16) for c in chains}
        kb = {c: kk[c].astype(BF16) for c in chains}
        att = {(d, h): jnp.where(level[d] == 0,
                                 lax.dot_general(qb[d, h], kb[d, h], NT_DIMS, preferred_element_type=F32), 0.0)
               for d, h in chains}
        for lv in range(1, n_levels + 1):
            half = 1 << (lv - 1)
            for d, h in chains:
                boundary = _block_row(g[d, h], 2 * half, half if d == 1 else half - 1)
                e = jnp.exp2(-jnp.abs(g[d, h] - boundary)).astype(BF16)
                p = lax.dot_general(qb[d, h] * e, kb[d, h] * e, NT_DIMS, preferred_element_type=F32)
                att[d, h] = jnp.where(level[d] == lv, p, att[d, h])
        for d, h in chains:
            o = jnp.dot(att[d, h].astype(BF16), ins[d][2][:, lanes[h]], preferred_element_type=F32)
            o = o + lax.dot_general((qq[d, h] * jnp.exp2(g[d, h])).astype(BF16), s_refs[d][h].astype(BF16),
                                    NT_DIMS, preferred_element_type=F32)
            o_refs[d][:, lanes[h]] = o
    for d, h in chains:
        g_last = g[d, h][last[d]:last[d] + 1, :]
        kx = (kk[d, h] * jnp.exp2(g_last - g[d, h])).astype(BF16)
        s_refs[d][h] = jnp.exp2(g_last) * s_refs[d][h] + lax.dot_general(
            ins[d][2][:, lanes[h]], kx, TN_DIMS, preferred_element_type=F32)


def _hgrn_scan(p_b, p_f, lb_f, lb_b, s0_f, s0_b, with_output):
    bsz, t, _ = p_b.shape
    c = HG_CHUNK
    nc = t // c
    cmaps = (lambda ci: ci, lambda ci: nc - 1 - ci)
    col = lambda d, blk: (lambda bi, ci: (bi, cmaps[d](ci), blk))
    state_spec = pl.BlockSpec((None, N_HEADS, HEAD_W, HEAD_W), lambda bi, ci: (bi, 0, 0, 0))
    state_shape = jax.ShapeDtypeStruct((bsz, N_HEADS, HEAD_W, HEAD_W), F32)
    in_specs, args = [], []
    for d, (lb, s0) in enumerate(((lb_f, s0_f), (lb_b, s0_b))):
        in_specs += [
            pl.BlockSpec((None, c, GROUP_W), col(d, PB_HQ)),
            pl.BlockSpec((None, c, GROUP_W), col(d, d)),
            pl.BlockSpec((None, c, GROUP_W), col(d, PB_HI)),
            pl.BlockSpec((N_HEADS, HEAD_W), lambda bi, ci: (0, 0)),
            state_spec,
        ]
        args += [p_b, p_f, p_b, lb, s0]
    out_shape, out_specs = [state_shape, state_shape], [state_spec, state_spec]
    if with_output:
        out_shape = [jax.ShapeDtypeStruct((bsz, t, GROUP_W), F32)] * 2 + out_shape
        out_specs = [pl.BlockSpec((None, c, GROUP_W), col(d, 0)) for d in range(2)] + out_specs
    res = pl.pallas_call(
        functools.partial(_hgrn_kernel, with_output=with_output),
        out_shape=out_shape,
        grid=(bsz, nc),
        in_specs=in_specs,
        out_specs=out_specs,
        compiler_params=_params("parallel", "arbitrary"),
        name="hgrn",
    )(*args)
    return tuple(res) if with_output else (None, None, res[0], res[1])


def _merge_kernel(x_ref, mod_ref, a_ref, of_ref, ob_ref, gt_ref, dn_ref, hn_ref, w_ref, g_ref, b_ref, o_ref,
                  *, da_scale):
    a = a_ref[...]
    o = of_ref[...] + ob_ref[...]
    parts_a, parts_o = [], []
    for h in range(N_HEADS):
        sl = slice(h * HEAD_W, (h + 1) * HEAD_W)
        parts_a.append(_rms_norm(a[:, sl], dn_ref[:, sl]) * da_scale)
        parts_o.append(_rms_norm(o[:, sl], hn_ref[:, sl]))
    ya = jnp.concatenate(parts_a, axis=1)
    yo = jnp.concatenate(parts_o, axis=1) * _silu(gt_ref[...].astype(F32))
    y = jnp.dot(jnp.concatenate([ya, yo], axis=1).astype(BF16), w_ref[...], preferred_element_type=F32)
    z = ALPHA * x_ref[...] + mod_ref[5:6, :] * y
    o_ref[...] = _layer_norm(z, g_ref[1:2, :], b_ref[1:2, :])


def _merge(x, mods, layer, mod_row, attn, o_f, o_b, p_b, da_norm, hg_norm, w_out, j, ln_g, ln_b, da_scale):
    bsz, t, d = x.shape
    tm = _row_tile(t)
    grp = pl.BlockSpec((None, tm, GROUP_W), lambda bi, i: (bi, i, 0))
    vec = lambda n: pl.BlockSpec((None, 1, n), lambda bi, i: (j, 0, 0))
    ln = pl.BlockSpec((None, 3, d), lambda bi, i: (layer, 0, 0))
    return pl.pallas_call(
        functools.partial(_merge_kernel, da_scale=da_scale),
        out_shape=jax.ShapeDtypeStruct(x.shape, F32),
        grid=(bsz, t // tm),
        in_specs=[
            pl.BlockSpec((None, tm, d), lambda bi, i: (bi, i, 0)),
            _mod_spec(layer, mod_row, 2),
            grp, grp, grp,
            pl.BlockSpec((None, tm, GROUP_W), lambda bi, i: (bi, i, PB_HG)),
            vec(GROUP_W), vec(GROUP_W),
            pl.BlockSpec((None, d, d), lambda bi, i: (j, 0, 0)),
            ln, ln,
        ],
        out_specs=pl.BlockSpec((None, tm, d), lambda bi, i: (bi, i, 0)),
        compiler_params=_params("parallel", "parallel"),
        name="even_merge",
    )(x, mods, attn, o_f, o_b, p_b, da_norm, hg_norm, w_out, ln_g, ln_b)


def _glu_kernel(x_ref, mod_ref, w_ref, b_ref, o_ref):
    h = (x_ref[...] * (1.0 + mod_ref[4:5, :]) + mod_ref[3:4, :]).astype(BF16)
    y = jnp.dot(h, w_ref[...], preferred_element_type=F32) + b_ref[...]
    d = o_ref.shape[-1]
    o_ref[...] = y[:, :d] * jax.nn.sigmoid(y[:, d:])


def _glu(x, mods, layer, mod_row, w_pw1, b_pw1, j):
    bsz, t, d = x.shape
    tm = _row_tile(t)
    return pl.pallas_call(
        _glu_kernel,
        out_shape=jax.ShapeDtypeStruct(x.shape, F32),
        grid=(bsz, t // tm),
        in_specs=[
            pl.BlockSpec((None, tm, d), lambda bi, i: (bi, i, 0)),
            _mod_spec(layer, mod_row, 2),
            pl.BlockSpec((None, d, 2 * d), lambda bi, i: (j, 0, 0)),
            pl.BlockSpec((None, 1, 2 * d), lambda bi, i: (j, 0, 0)),
        ],
        out_specs=pl.BlockSpec((None, tm, d), lambda bi, i: (bi, i, 0)),
        compiler_params=_params("parallel", "parallel"),
        name="conv_glu",
    )(x, mods, w_pw1, b_pw1)


def _conv_kernel(x_ref, mod_ref, u_ref, up_ref, un_ref, wdw_ref, bdw_ref, cg_ref, cb_ref, w_ref, b2_ref,
                 g_ref, b_ref, o_ref, ext_ref, sh_ref, acc_ref):
    i = pl.program_id(1)
    tm = u_ref.shape[0]
    ext_ref[0:CONV_HALO, :] = jnp.where(i > 0, up_ref[...], 0.0)
    ext_ref[CONV_HALO:CONV_HALO + tm, :] = u_ref[...]
    ext_ref[CONV_HALO + tm:, :] = jnp.where(i < pl.num_programs(1) - 1, un_ref[...], 0.0)
    off = CONV_HALO - CONV_WIDTH // 2
    span = sh_ref.shape[1]
    for cb in range(u_ref.shape[1] // HEAD_W):
        cols = slice(cb * HEAD_W, (cb + 1) * HEAD_W)
        for p in range(1, 8):
            sh_ref[p - 1] = ext_ref[p:p + span, cols]

        def row_block(rb, carry):
            r0 = pl.multiple_of(rb * CONV_ROWS, CONV_ROWS)
            acc = None
            for w in range(CONV_WIDTH):
                p, a = (off + w) % 8, (off + w) // 8
                rows = pl.ds(r0 + 8 * a, CONV_ROWS)
                win = ext_ref[rows, cols] if p == 0 else sh_ref[p - 1, rows, :]
                term = win * wdw_ref[w:w + 1, cols]
                acc = term if acc is None else acc + term
            acc_ref[pl.ds(r0, CONV_ROWS), cols] = acc
            return carry

        lax.fori_loop(0, tm // CONV_ROWS, row_block, 0)
    u = _silu(_layer_norm(acc_ref[...] + bdw_ref[...], cg_ref[...], cb_ref[...]))
    y = jnp.dot(u.astype(BF16), w_ref[...], preferred_element_type=F32) + b2_ref[...]
    z = ALPHA * x_ref[...] + mod_ref[5:6, :] * y
    o_ref[...] = _layer_norm(z, g_ref[1:2, :], b_ref[1:2, :])


def _conv(x, mods, layer, mod_row, u, w_dw, b_dw, cg, cb, w_pw2, b_pw2, j, ln_g, ln_b):
    bsz, t, d = x.shape
    tm = _row_tile(t)
    r = tm // CONV_HALO
    n_halo = t // CONV_HALO
    row = pl.BlockSpec((None, tm, d), lambda bi, i: (bi, i, 0))
    vec = pl.BlockSpec((None, 1, d), lambda bi, i: (j, 0, 0))
    ln = pl.BlockSpec((None, 3, d), lambda bi, i: (layer, 0, 0))
    return pl.pallas_call(
        _conv_kernel,
        out_shape=jax.ShapeDtypeStruct(x.shape, F32),
        grid=(bsz, t // tm),
        in_specs=[
            row,
            _mod_spec(layer, mod_row, 2),
            row,
            pl.BlockSpec((None, CONV_HALO, d), lambda bi, i: (bi, jnp.maximum(i * r - 1, 0), 0)),
            pl.BlockSpec((None, CONV_HALO, d), lambda bi, i: (bi, jnp.minimum((i + 1) * r, n_halo - 1), 0)),
            pl.BlockSpec((None, CONV_WIDTH, d), lambda bi, i: (j, 0, 0)),
            vec, vec, vec,
            pl.BlockSpec((None, d, d), lambda bi, i: (j, 0, 0)),
            vec, ln, ln,
        ],
        out_specs=row,
        scratch_shapes=[pltpu.VMEM((tm + 2 * CONV_HALO, d), F32),
                        pltpu.VMEM((7, tm + 8 * ((CONV_WIDTH + 7) // 8 - 1), HEAD_W), F32),
                        pltpu.VMEM((tm, d), F32)],
        compiler_params=_params("parallel", "parallel"),
        name="conv_mix",
    )(x, mods, u, u, u, w_dw, b_dw, cg, cb, w_pw2, b_pw2, ln_g, ln_b)


def kernel(x, c, ctx, c_ctx, w_mod, b_mod, ln_g, ln_b, ffn_w13, ffn_w2, ev_w_in, ev_w_out, da_lambda, da_norm,
           hg_lb, hg_norm, cv_w_pw1, cv_b_pw1, cv_w_dw, cv_b_dw, cv_ln_g, cv_ln_b, cv_w_pw2, cv_b_pw2):
    bsz, seq, d = x.shape
    per_layer_row = lambda a: a.reshape(a.shape[0], 1, a.shape[1])

    cc = jnp.zeros((16, d), F32).at[:bsz].set(c).at[bsz].set(c_ctx)
    mods = _mods(cc, w_mod, b_mod).reshape(DEPTH, 16, N_MOD, d)
    lat_row = lambda bi: bi
    ctx_row = lambda bi: bsz
    rope_tabs = _rope_tables(seq)
    lb_w = jax.nn.softmax(hg_lb.astype(F32), axis=0)
    lower_bounds = jnp.cumsum(lb_w, axis=0) - lb_w[:1]

    w13, w2 = ffn_w13.astype(BF16), ffn_w2.astype(BF16)
    w_in, w_out = ev_w_in.astype(BF16), ev_w_out.astype(BF16)
    w_pw1, w_pw2 = cv_w_pw1.astype(BF16), cv_w_pw2.astype(BF16)
    da_norm3, hg_norm3 = per_layer_row(da_norm), per_layer_row(hg_norm)
    conv_vecs = [per_layer_row(a) for a in (cv_b_dw, cv_ln_g, cv_ln_b)]
    b_pw1, b_pw2 = per_layer_row(cv_b_pw1), per_layer_row(cv_b_pw2)

    xl, xc = x, ctx
    for layer in range(DEPTH):
        j = layer // 2
        even = layer % 2 == 0
        ctx_pre = layer <= LAST_CTX_READ
        ctx_full = layer < LAST_CTX_READ
        ffn = lambda xx, mrow, fi, sub: _ffn(xx, mods, layer, mrow, w13, w2, fi, ln_g, ln_b, sub)

        xl = ffn(xl, lat_row, 0, 0)
        if ctx_pre:
            xc = ffn(xc, ctx_row, 0, 0)

        if even:
            lam_init = 0.8 - 0.6 * math.exp(-0.3 * layer)
            lv = da_lambda[j].astype(F32)
            lam = (jnp.exp(jnp.sum(lv[0] * lv[1])) - jnp.exp(jnp.sum(lv[2] * lv[3])) + lam_init).reshape(1)
            lb_f = lower_bounds[j, :GROUP_W].reshape(N_HEADS, HEAD_W)
            lb_b = lower_bounds[j, GROUP_W:].reshape(N_HEADS, HEAD_W)
            pb_l, pf_l = _inproj(xl, mods, layer, lat_row, w_in, j, rope_tabs)
            pb_c, pf_c = _inproj(xc, mods, layer, ctx_row, w_in, j, None)
            a_l = _attention(lam, pb_l, [pb_l, pb_c])
            zeros = jnp.zeros((bsz, N_HEADS, HEAD_W, HEAD_W), F32)
            ocf, ocb, s_f, s_b = _hgrn_scan(pb_c, pf_c, lb_f, lb_b, zeros, zeros, ctx_full)
            olf, olb, _, _ = _hgrn_scan(pb_l, pf_l, lb_f, lb_b, s_f, s_b, True)
            merge = lambda xx, mrow, a, o_f, o_b, pb: _merge(
                xx, mods, layer, mrow, a, o_f, o_b, pb, da_norm3, hg_norm3, w_out, j, ln_g, ln_b, 1.0 - lam_init)
            if ctx_full:
                a_c = _attention(lam, pb_c, [pb_c])
                xc = merge(xc, ctx_row, a_c, ocf, ocb, pb_c)
            xl = merge(xl, lat_row, a_l, olf, olb, pb_l)
        else:
            conv = lambda xx, mrow, u: _conv(xx, mods, layer, mrow, u, cv_w_dw, *conv_vecs, w_pw2, b_pw2, j,
                                             ln_g, ln_b)
            if ctx_full:
                xc = conv(xc, ctx_row, _glu(xc, mods, layer, ctx_row, w_pw1, b_pw1, j))
            xl = conv(xl, lat_row, _glu(xl, mods, layer, lat_row, w_pw1, b_pw1, j))

        xl = ffn(xl, lat_row, 1, 2)
        if ctx_full:
            xc = ffn(xc, ctx_row, 1, 2)
    return xl
```

```python
import functools
import math

import jax
import jax.numpy as jnp
from jax import lax
from jax.experimental import pallas as pl
from jax.experimental.pallas import tpu as pltpu

F32 = jnp.float32
BF16 = jnp.bfloat16

D_MODEL = 1024
DEPTH = 4
GRID_W = 64
N_HEADS = 4
HEAD_W = 128
QK_DIM = 64
GROUP_W = N_HEADS * HEAD_W
D_FF = 2816
N_MOD = 9
CONV_WIDTH = 31
CONV_HALO = 16
ROPE_BASE = 10000.0
LAST_CTX_READ = 2 * ((DEPTH - 1) // 2)
ALPHA = (2 * DEPTH) ** 0.25
EPS = 1e-5
HG_CHUNK = 128
MXU_WIDTH = 256
FFN_SPLIT = (D_FF // MXU_WIDTH + 1) // 2 * MXU_WIDTH
FFN_CHUNKS = ((0, FFN_SPLIT), (FFN_SPLIT, D_FF))
ROW_TILE = 512
ATTN_ROW_TILE = 1024
ATTN_SUB_ROWS = 256
CONV_ROWS = 64
VMEM_LIMIT = 56 * 1024 * 1024

PB_WIDTH = 6 * GROUP_W
PF_WIDTH = 2 * GROUP_W
PB_Q, PB_K, PB_V, PB_HQ, PB_HI, PB_HG = range(6)

NT_DIMS = (((1,), (1,)), ((), ()))
TN_DIMS = (((0,), (0,)), ((), ()))


def _params(*sem):
    return pltpu.CompilerParams(dimension_semantics=sem, vmem_limit_bytes=VMEM_LIMIT)


def _layer_norm(z, g, b):
    mu = jnp.mean(z, axis=-1, keepdims=True)
    zc = z - mu
    var = jnp.mean(zc * zc, axis=-1, keepdims=True)
    return zc * lax.rsqrt(var + EPS) * g + b


def _rms_norm(z, g):
    return z * lax.rsqrt(jnp.mean(z * z, axis=-1, keepdims=True) + EPS) * g


def _silu(a):
    return a * jax.nn.sigmoid(a)


def _row_tile(t):
    return min(t, ROW_TILE)


def _mod_spec(layer, mod_row, grid_rank):
    if grid_rank == 2:
        return pl.BlockSpec((None, None, N_MOD, D_MODEL), lambda bi, i: (layer, mod_row(bi), 0, 0))
    return pl.BlockSpec((None, None, N_MOD, D_MODEL), lambda bi, i, j: (layer, mod_row(bi), 0, 0))


def _mods_kernel(c_ref, w_ref, b_ref, o_ref):
    a = _silu(c_ref[...]).astype(BF16)
    o_ref[...] = jnp.dot(a, w_ref[...].astype(BF16), preferred_element_type=F32) + b_ref[...]


def _mods(cc, w_mod, b_mod):
    n = cc.shape[0]
    tn = 1024
    return pl.pallas_call(
        _mods_kernel,
        out_shape=jax.ShapeDtypeStruct((DEPTH, n, N_MOD * D_MODEL), F32),
        grid=(DEPTH, N_MOD * D_MODEL // tn),
        in_specs=[
            pl.BlockSpec((n, D_MODEL), lambda l, j: (0, 0)),
            pl.BlockSpec((None, D_MODEL, tn), lambda l, j: (l, 0, j)),
            pl.BlockSpec((None, 1, tn), lambda l, j: (l, 0, j)),
        ],
        out_specs=pl.BlockSpec((None, n, tn), lambda l, j: (l, 0, j)),
        compiler_params=_params("parallel", "parallel"),
        name="mods",
    )(cc, w_mod, b_mod.reshape(DEPTH, 1, N_MOD * D_MODEL))


def _ffn_kernel(xn_ref, xp_ref, modn_ref, modp_ref, w13_ref, w2_ref, g_ref, b_ref, o_ref, h_scr, acc_scr,
                *, sub, n_tiles):
    s = pl.program_id(0)

    def prologue(dst):
        shift = modn_ref[3 * sub:3 * sub + 1, :]
        scale = modn_ref[3 * sub + 1:3 * sub + 2, :]
        h_scr[dst] = (xn_ref[...] * (1.0 + scale) + shift).astype(BF16)

    def epilogue(src):
        gate = modp_ref[3 * sub + 2:3 * sub + 3, :]
        z = ALPHA * xp_ref[...] + gate * (0.5 * acc_scr[src])
        o_ref[...] = _layer_norm(z, g_ref[sub:sub + 1, :], b_ref[sub:sub + 1, :])

    def matmuls(slot):
        h = h_scr[slot]
        ab = [(jnp.dot(h, w13_ref[:, lo:hi], preferred_element_type=F32),
               jnp.dot(h, w13_ref[:, D_FF + lo:D_FF + hi], preferred_element_type=F32)) for lo, hi in FFN_CHUNKS]
        ys = [jnp.dot((_silu(a) * b).astype(BF16), w2_ref[lo:hi, :], preferred_element_type=F32)
              for (a, b), (lo, hi) in zip(ab, FFN_CHUNKS)]
        acc_scr[slot] = functools.reduce(jnp.add, ys)

    @pl.when(s == 0)
    def _():
        prologue(0)
        acc_scr[1] = jnp.zeros(acc_scr.shape[1:], F32)

    for parity in range(2):
        @pl.when((s >= 1) & (s <= n_tiles) & (s % 2 == parity))
        def _():
            epilogue(parity)
            prologue(parity)
            matmuls(1 - parity)

    @pl.when(s == n_tiles + 1)
    def _():
        epilogue((n_tiles + 1) % 2)


def _ffn(x, mods, layer, mod_row, w13, w2, fi, ln_g, ln_b, sub):
    bsz, t, d = x.shape
    tm = _row_tile(t)
    per_b = t // tm
    n = bsz * per_b
    xf = x.reshape(bsz * t, d)
    nxt = lambda s: jnp.minimum(s, n - 1)
    prv = lambda s: jnp.clip(s - 2, 0, n - 1)
    resident = dict(pipeline_mode=pl.Buffered(1))
    out = pl.pallas_call(
        functools.partial(_ffn_kernel, sub=sub, n_tiles=n),
        out_shape=jax.ShapeDtypeStruct(xf.shape, F32),
        grid=(n + 2,),
        in_specs=[
            pl.BlockSpec((tm, d), lambda s: (nxt(s), 0)),
            pl.BlockSpec((tm, d), lambda s: (prv(s), 0)),
            pl.BlockSpec((None, None, N_MOD, d), lambda s: (layer, mod_row(nxt(s) // per_b), 0, 0)),
            pl.BlockSpec((None, None, N_MOD, d), lambda s: (layer, mod_row(prv(s) // per_b), 0, 0)),
            pl.BlockSpec((None, None, d, 2 * D_FF), lambda s: (layer, fi, 0, 0), **resident),
            pl.BlockSpec((None, None, D_FF, d), lambda s: (layer, fi, 0, 0), **resident),
            pl.BlockSpec((None, 3, d), lambda s: (layer, 0, 0)),
            pl.BlockSpec((None, 3, d), lambda s: (layer, 0, 0)),
        ],
        out_specs=pl.BlockSpec((tm, d), lambda s: (prv(s), 0)),
        scratch_shapes=[pltpu.VMEM((2, tm, d), BF16), pltpu.VMEM((2, tm, d), F32)],
        compiler_params=_params("arbitrary"),
        name="ffn",
    )(xf, xf, mods, mods, w13, w2, ln_g, ln_b)
    return out.reshape(x.shape)


def _inproj_kernel(*refs, rope):
    if rope:
        x_ref, mod_ref, w_ref, cos_ref, sa_ref, sb_ref, ob_ref, of_ref = refs
    else:
        x_ref, mod_ref, w_ref, ob_ref, of_ref = refs
    h = (x_ref[...] * (1.0 + mod_ref[4:5, :]) + mod_ref[3:4, :]).astype(BF16)
    w2 = 2 * GROUP_W
    proj = lambda grp: jnp.dot(h, w_ref[:, grp * w2:(grp + 1) * w2], preferred_element_type=F32)
    y = proj(0)
    if rope:
        wide = lambda r: jnp.concatenate([r[...]] * (w2 // HEAD_W), axis=1)
        y = (y * wide(cos_ref) + pltpu.roll(y, w2 - QK_DIM // 4, 1) * wide(sa_ref)
             + pltpu.roll(y, QK_DIM // 4, 1) * wide(sb_ref))
    ob_ref[:, 0:w2] = y.astype(BF16)
    ob_ref[:, w2:2 * w2] = proj(1).astype(BF16)
    of_ref[...] = proj(2)
    ob_ref[:, 2 * w2:3 * w2] = proj(3).astype(BF16)


def _inproj(x, mods, layer, mod_row, w_in, j, rope_tabs):
    bsz, t, d = x.shape
    tm = _row_tile(t)
    rope = rope_tabs is not None
    in_specs = [
        pl.BlockSpec((None, tm, d), lambda bi, i: (bi, i, 0)),
        _mod_spec(layer, mod_row, 2),
        pl.BlockSpec((None, d, 8 * GROUP_W), lambda bi, i: (j, 0, 0)),
    ]
    args = [x, mods, w_in]
    if rope:
        in_specs += [pl.BlockSpec((tm, HEAD_W), lambda bi, i: (i, 0))] * 3
        args += list(rope_tabs)
    return pl.pallas_call(
        functools.partial(_inproj_kernel, rope=rope),
        out_shape=[jax.ShapeDtypeStruct((bsz, t, PB_WIDTH), BF16), jax.ShapeDtypeStruct((bsz, t, PF_WIDTH), F32)],
        grid=(bsz, t // tm),
        in_specs=in_specs,
        out_specs=[pl.BlockSpec((None, tm, PB_WIDTH), lambda bi, i: (bi, i, 0)),
                   pl.BlockSpec((None, tm, PF_WIDTH), lambda bi, i: (bi, i, 0))],
        compiler_params=_params("parallel", "parallel"),
        name="inproj",
    )(*args)


def _rope_tables(n_tokens):
    pos = jnp.arange(n_tokens)
    half = QK_DIM // 2
    inv_freq = ROPE_BASE ** (-jnp.arange(0, half, 2, dtype=F32) / half)

    def table(p):
        ang = p.astype(F32)[:, None] * inv_freq[None, :]
        ang = jnp.concatenate([ang, ang], axis=-1)
        return jnp.cos(ang), jnp.sin(ang)

    (cos_r, sin_r), (cos_c, sin_c) = table(pos // GRID_W), table(pos % GRID_W)
    cos = jnp.concatenate([cos_r, cos_c, cos_r, cos_c], axis=-1)
    sin = jnp.concatenate([sin_r, sin_c, sin_r, sin_c], axis=-1)
    first = (jnp.arange(HEAD_W) % half) < half // 2
    return cos, jnp.where(first, -sin, 0.0), jnp.where(first, 0.0, sin)


def _emit_interleaved(*stage_lists):
    live = list(stage_lists)
    while live:
        for stages in list(live):
            if next(stages, StopIteration) is StopIteration:
                live.remove(stages)


def _attn_stages(lam_ref, q_ref, kv_refs, o_ref, n_sub):
    n_src = len(kv_refs) // 2
    rows = q_ref.shape[0] // n_sub
    lane = lax.broadcasted_iota(jnp.int32, (1, HEAD_W), 1)
    ks = [kv_refs[2 * i][...] for i in range(n_src)]
    vs = [jnp.concatenate([kv_refs[2 * i + 1][...], jnp.ones(kv_refs[2 * i + 1].shape, BF16)], axis=1)
          for i in range(n_src)]
    units = [(sub, comp) for sub in range(n_sub) for comp in range(2)]

    def scores(unit):
        sub, comp = unit
        q = q_ref[sub * rows:(sub + 1) * rows, :] * (QK_DIM ** -0.5)
        sel = (lane < QK_DIM) if comp == 0 else (lane >= QK_DIM)
        qc = jnp.where(sel, q, jnp.zeros_like(q))
        return [lax.dot_general(qc, k, NT_DIMS, preferred_element_type=F32) for k in ks]

    def softmax_numerators(ss):
        m = functools.reduce(jnp.maximum, [jnp.max(s, axis=-1, keepdims=True) for s in ss])
        return [jnp.exp((s - m).astype(BF16)) for s in ss]

    def mix(ps):
        acc = functools.reduce(jnp.add, [jnp.dot(p, v, preferred_element_type=F32) for p, v in zip(ps, vs)])
        return acc[:, :HEAD_W] / acc[:, HEAD_W:HEAD_W + 1]

    n = len(units)
    ss, ps, mixed = {}, {}, {}
    for t in range(n + 2):
        if t < n:
            ss[t] = scores(units[t])
        if 0 <= t - 2 < n:
            mixed[units[t - 2]] = mix(ps.pop(t - 2))
        if 0 <= t - 1 < n:
            ps[t - 1] = softmax_numerators(ss.pop(t - 1))
        yield
    lam = lam_ref[0]
    for sub in range(n_sub):
        o_ref[sub * rows:(sub + 1) * rows, :] = mixed[(sub, 0)] - lam * mixed[(sub, 1)]
    yield


def _attn_kernel(lam_ref, q_ref, *refs, n_sub):
    _emit_interleaved(_attn_stages(lam_ref, q_ref, refs[:-1], refs[-1], n_sub))


def _attention(lam, p_q, kv_sources):
    bsz, t, _ = p_q.shape
    tq = min(t, ATTN_ROW_TILE)
    n_src = len(kv_sources)
    in_specs = [
        pl.BlockSpec(memory_space=pltpu.SMEM),
        pl.BlockSpec((None, tq, HEAD_W), lambda bi, h, i: (bi, i, PB_Q * N_HEADS + h)),
    ]
    args = [lam, p_q]
    for src in kv_sources:
        tk = src.shape[1]
        in_specs.append(pl.BlockSpec((None, tk, HEAD_W), lambda bi, h, i: (bi, 0, PB_K * N_HEADS + h)))
        in_specs.append(pl.BlockSpec((None, tk, HEAD_W), lambda bi, h, i: (bi, 0, PB_V * N_HEADS + h)))
        args += [src, src]
    return pl.pallas_call(
        functools.partial(_attn_kernel, n_sub=max(tq // ATTN_SUB_ROWS, 1)),
        out_shape=jax.ShapeDtypeStruct((bsz, t, GROUP_W), F32),
        grid=(bsz, N_HEADS, t // tq),
        in_specs=in_specs,
        out_specs=pl.BlockSpec((None, tq, HEAD_W), lambda bi, h, i: (bi, i, h)),
        compiler_params=_params("parallel", "parallel", "arbitrary"),
        name="diff_attn",
    )(*args)


def _time_cumsum(x, tri):
    hi = x.astype(BF16)
    r1 = x - hi.astype(F32)
    mid = r1.astype(BF16)
    lo = (r1 - mid.astype(F32)).astype(BF16)
    parts = [jnp.dot(tri, part, preferred_element_type=F32) for part in (hi, mid, lo)]
    return parts[0] + (parts[1] + parts[2])


def _block_row(g, blk, idx):
    n, k = g.shape
    if blk >= 8:
        g4 = g.reshape(n // blk, blk // 8, 8, k)
        ref = g4[:, idx // 8:idx // 8 + 1, idx % 8:idx % 8 + 1, :]
        return jnp.broadcast_to(ref, g4.shape).reshape(n, k)
    g3 = g.reshape(n // 8, 8, k)
    sub = lax.broadcasted_iota(jnp.int32, (1, 8, 1), 1) // blk
    out = None
    for j in range(8 // blk):
        ref = jnp.broadcast_to(g3[:, j * blk + idx:j * blk + idx + 1, :], g3.shape)
        out = ref if out is None else jnp.where(sub == j, ref, out)
    return out.reshape(n, k)


def _hgrn_refs(refs, with_output):
    ins = [refs[0:5], refs[5:10]]
    outs = refs[10:]
    o_refs, s_refs = (outs[0:2], outs[2:4]) if with_output else (None, outs[0:2])
    return ins, o_refs, s_refs


def _hgrn_load_state(ins, s_refs, chunk_axis):
    @pl.when(pl.program_id(chunk_axis) == 0)
    def _():
        for d in range(2):
            s_refs[d][...] = ins[d][4][...]


def _hgrn_kernel(*refs, with_output):
    ins, o_refs, s_refs = _hgrn_refs(refs, with_output)
    _hgrn_load_state(ins, s_refs, 1)
    _emit_interleaved(_hgrn_stages(ins, o_refs, s_refs))


def _hgrn_stages(ins, o_refs, s_refs):
    with_output = o_refs is not None
    n = ins[0][1].shape[0]
    n_levels = n.bit_length() - 1
    rt = lax.broadcasted_iota(jnp.int32, (n, n), 0)
    cs = lax.broadcasted_iota(jnp.int32, (n, n), 1)
    tri = [(cs <= rt).astype(BF16), (cs >= rt).astype(BF16)]
    last = [n - 1, 0]
    if with_output:
        differ = rt ^ cs
        top_bit = functools.reduce(jnp.add, [(differ >= (1 << b)).astype(jnp.int32) for b in range(1, n_levels)])
        level = [jnp.where(earlier, top_bit + 1, jnp.where(rt == cs, 0, -1)) for earlier in (cs < rt, cs > rt)]
    chains = [(d, h) for d in range(2) for h in range(N_HEADS)]
    lanes = [slice(h * HEAD_W, (h + 1) * HEAD_W) for h in range(N_HEADS)]
    kk, g = {}, {}
    for d, h in chains:
        lb = ins[d][3][h:h + 1, :]
        f = lb + (1.0 - lb) * jax.nn.sigmoid(ins[d][1][:, lanes[h]])
        kk[d, h] = 1.0 - f
        g[d, h] = _time_cumsum(jnp.log2(f), tri[d])
    yield
    if with_output:
        qq = {(d, h): _silu(ins[d][0][:, lanes[h]].astype(F32)) for d, h in chains}
        qb = {c: qq[c].astype(BF16) for c in chains}
        kb = {c: kk[c].astype(BF16) for c in chains}
        att = {(d, h): jnp.where(level[d] == 0,
                                 lax.dot_general(qb[d, h], kb[d, h], NT_DIMS, preferred_element_type=F32), 0.0)
               for d, h in chains}
        for lv in range(1, n_levels + 1):
            half = 1 << (lv - 1)
            for d, h in chains:
                boundary = _block_row(g[d, h], 2 * half, half if d == 1 else half - 1)
                e = jnp.exp2(-jnp.abs(g[d, h] - boundary)).astype(BF16)
                p = lax.dot_general(qb[d, h] * e, kb[d, h] * e, NT_DIMS, preferred_element_type=F32)
                att[d, h] = jnp.where(level[d] == lv, p, att[d, h])
            yield
        for d, h in chains:
            o = jnp.dot(att[d, h].astype(BF16), ins[d][2][:, lanes[h]], preferred_element_type=F32)
            o = o + lax.dot_general((qq[d, h] * jnp.exp2(g[d, h])).astype(BF16), s_refs[d][h].astype(BF16),
                                    NT_DIMS, preferred_element_type=F32)
            o_refs[d][:, lanes[h]] = o
        yield
    for d, h in chains:
        g_last = g[d, h][last[d]:last[d] + 1, :]
        kx = (kk[d, h] * jnp.exp2(g_last - g[d, h])).astype(BF16)
        s_refs[d][h] = jnp.exp2(g_last) * s_refs[d][h] + lax.dot_general(
            ins[d][2][:, lanes[h]], kx, TN_DIMS, preferred_element_type=F32)


def _hgrn_operands(p_b, p_f, lb_f, lb_b, s0_f, s0_b, with_output):
    bsz, t, _ = p_b.shape
    c = HG_CHUNK
    nc = t // c
    cmaps = (lambda ci: ci, lambda ci: nc - 1 - ci)
    col = lambda d, blk: (lambda bi, ci: (bi, cmaps[d](ci), blk))
    state_spec = pl.BlockSpec((None, N_HEADS, HEAD_W, HEAD_W), lambda bi, ci: (bi, 0, 0, 0))
    state_shape = jax.ShapeDtypeStruct((bsz, N_HEADS, HEAD_W, HEAD_W), F32)
    in_specs, args = [], []
    for d, (lb, s0) in enumerate(((lb_f, s0_f), (lb_b, s0_b))):
        in_specs += [
            pl.BlockSpec((None, c, GROUP_W), col(d, PB_HQ)),
            pl.BlockSpec((None, c, GROUP_W), col(d, d)),
            pl.BlockSpec((None, c, GROUP_W), col(d, PB_HI)),
            pl.BlockSpec((N_HEADS, HEAD_W), lambda bi, ci: (0, 0)),
            state_spec,
        ]
        args += [p_b, p_f, p_b, lb, s0]
    out_shape, out_specs = [state_shape, state_shape], [state_spec, state_spec]
    if with_output:
        out_shape = [jax.ShapeDtypeStruct((bsz, t, GROUP_W), F32)] * 2 + out_shape
        out_specs = [pl.BlockSpec((None, c, GROUP_W), col(d, 0)) for d in range(2)] + out_specs
    return in_specs, args, out_shape, out_specs


def _hgrn_scan(p_b, p_f, lb_f, lb_b, s0_f, s0_b, with_output):
    in_specs, args, out_shape, out_specs = _hgrn_operands(p_b, p_f, lb_f, lb_b, s0_f, s0_b, with_output)
    res = pl.pallas_call(
        functools.partial(_hgrn_kernel, with_output=with_output),
        out_shape=out_shape,
        grid=(p_b.shape[0], p_b.shape[1] // HG_CHUNK),
        in_specs=in_specs,
        out_specs=out_specs,
        compiler_params=_params("parallel", "arbitrary"),
        name="hgrn",
    )(*args)
    return tuple(res) if with_output else (None, None, res[0], res[1])


def _mixer_kernel(*refs, n_attn_in, n_sub):
    attn_in, hgrn_in, outs = refs[:n_attn_in], refs[n_attn_in:n_attn_in + 10], refs[n_attn_in + 10:]
    ins, o_refs, s_refs = _hgrn_refs(hgrn_in + outs[1:], True)
    _hgrn_load_state(ins, s_refs, 1)
    _emit_interleaved(_attn_stages(attn_in[0], attn_in[1], attn_in[2:], outs[0], n_sub),
                      _hgrn_stages(ins, o_refs, s_refs))


def _latent_mixer(lam, pb_l, pf_l, pb_c, lb_f, lb_b, s0_f, s0_b):
    bsz, t, _ = pb_l.shape
    nc = t // HG_CHUNK
    tq = t * N_HEADS // nc
    nq = t // tq
    head_col = lambda grp: (lambda bi, ci: (bi, 0, grp * N_HEADS + ci // nq))
    in_specs = [
        pl.BlockSpec(memory_space=pltpu.SMEM),
        pl.BlockSpec((None, tq, HEAD_W), lambda bi, ci: (bi, ci % nq, PB_Q * N_HEADS + ci // nq)),
    ]
    args = [lam, pb_l]
    for src in (pb_l, pb_c):
        in_specs += [pl.BlockSpec((None, src.shape[1], HEAD_W), head_col(PB_K)),
                     pl.BlockSpec((None, src.shape[1], HEAD_W), head_col(PB_V))]
        args += [src, src]
    n_attn_in = len(args)
    h_specs, h_args, h_shape, h_out_specs = _hgrn_operands(pb_l, pf_l, lb_f, lb_b, s0_f, s0_b, True)
    res = pl.pallas_call(
        functools.partial(_mixer_kernel, n_attn_in=n_attn_in, n_sub=max(tq // ATTN_SUB_ROWS, 1)),
        out_shape=[jax.ShapeDtypeStruct((bsz, t, GROUP_W), F32)] + h_shape,
        grid=(bsz, nc),
        in_specs=in_specs + h_specs,
        out_specs=[pl.BlockSpec((None, tq, HEAD_W), lambda bi, ci: (bi, ci % nq, ci // nq))] + h_out_specs,
        compiler_params=_params("parallel", "arbitrary"),
        name="latent_mixer",
    )(*args, *h_args)
    return res[0], res[1], res[2]


def _merge_kernel(x_ref, mod_ref, a_ref, of_ref, ob_ref, gt_ref, dn_ref, hn_ref, w_ref, g_ref, b_ref, o_ref,
                  *, da_scale):
    a = a_ref[...]
    o = of_ref[...] + ob_ref[...]
    parts_a, parts_o = [], []
    for h in range(N_HEADS):
        sl = slice(h * HEAD_W, (h + 1) * HEAD_W)
        parts_a.append(_rms_norm(a[:, sl], dn_ref[:, sl]) * da_scale)
        parts_o.append(_rms_norm(o[:, sl], hn_ref[:, sl]))
    ya = jnp.concatenate(parts_a, axis=1)
    yo = jnp.concatenate(parts_o, axis=1) * _silu(gt_ref[...].astype(F32))
    y = jnp.dot(jnp.concatenate([ya, yo], axis=1).astype(BF16), w_ref[...], preferred_element_type=F32)
    z = ALPHA * x_ref[...] + mod_ref[5:6, :] * y
    o_ref[...] = _layer_norm(z, g_ref[1:2, :], b_ref[1:2, :])


def _merge(x, mods, layer, mod_row, attn, o_f, o_b, p_b, da_norm, hg_norm, w_out, j, ln_g, ln_b, da_scale):
    bsz, t, d = x.shape
    tm = _row_tile(t)
    grp = pl.BlockSpec((None, tm, GROUP_W), lambda bi, i: (bi, i, 0))
    vec = lambda n: pl.BlockSpec((None, 1, n), lambda bi, i: (j, 0, 0))
    ln = pl.BlockSpec((None, 3, d), lambda bi, i: (layer, 0, 0))
    return pl.pallas_call(
        functools.partial(_merge_kernel, da_scale=da_scale),
        out_shape=jax.ShapeDtypeStruct(x.shape, F32),
        grid=(bsz, t // tm),
        in_specs=[
            pl.BlockSpec((None, tm, d), lambda bi, i: (bi, i, 0)),
            _mod_spec(layer, mod_row, 2),
            grp, grp, grp,
            pl.BlockSpec((None, tm, GROUP_W), lambda bi, i: (bi, i, PB_HG)),
            vec(GROUP_W), vec(GROUP_W),
            pl.BlockSpec((None, d, d), lambda bi, i: (j, 0, 0)),
            ln, ln,
        ],
        out_specs=pl.BlockSpec((None, tm, d), lambda bi, i: (bi, i, 0)),
        compiler_params=_params("parallel", "parallel"),
        name="even_merge",
    )(x, mods, attn, o_f, o_b, p_b, da_norm, hg_norm, w_out, ln_g, ln_b)


def _glu_kernel(x_ref, mod_ref, w_ref, b_ref, o_ref):
    h = (x_ref[...] * (1.0 + mod_ref[4:5, :]) + mod_ref[3:4, :]).astype(BF16)
    y = jnp.dot(h, w_ref[...], preferred_element_type=F32) + b_ref[...]
    d = o_ref.shape[-1]
    o_ref[...] = y[:, :d] * jax.nn.sigmoid(y[:, d:])


def _glu(x, mods, layer, mod_row, w_pw1, b_pw1, j):
    bsz, t, d = x.shape
    tm = _row_tile(t)
    return pl.pallas_call(
        _glu_kernel,
        out_shape=jax.ShapeDtypeStruct(x.shape, F32),
        grid=(bsz, t // tm),
        in_specs=[
            pl.BlockSpec((None, tm, d), lambda bi, i: (bi, i, 0)),
            _mod_spec(layer, mod_row, 2),
            pl.BlockSpec((None, d, 2 * d), lambda bi, i: (j, 0, 0)),
            pl.BlockSpec((None, 1, 2 * d), lambda bi, i: (j, 0, 0)),
        ],
        out_specs=pl.BlockSpec((None, tm, d), lambda bi, i: (bi, i, 0)),
        compiler_params=_params("parallel", "parallel"),
        name="conv_glu",
    )(x, mods, w_pw1, b_pw1)


def _conv_kernel(x_ref, mod_ref, u_ref, up_ref, un_ref, wdw_ref, bdw_ref, cg_ref, cb_ref, w_ref, b2_ref,
                 g_ref, b_ref, o_ref, ext_ref, sh_ref, acc_ref):
    i = pl.program_id(1)
    tm = u_ref.shape[0]
    ext_ref[0:CONV_HALO, :] = jnp.where(i > 0, up_ref[...], 0.0)
    ext_ref[CONV_HALO:CONV_HALO + tm, :] = u_ref[...]
    ext_ref[CONV_HALO + tm:, :] = jnp.where(i < pl.num_programs(1) - 1, un_ref[...], 0.0)
    off = CONV_HALO - CONV_WIDTH // 2
    span = sh_ref.shape[1]
    for cb in range(u_ref.shape[1] // HEAD_W):
        cols = slice(cb * HEAD_W, (cb + 1) * HEAD_W)
        for p in range(1, 8):
            sh_ref[p - 1] = ext_ref[p:p + span, cols]

        def row_block(rb, carry):
            r0 = pl.multiple_of(rb * CONV_ROWS, CONV_ROWS)
            acc = None
            for w in range(CONV_WIDTH):
                p, a = (off + w) % 8, (off + w) // 8
                rows = pl.ds(r0 + 8 * a, CONV_ROWS)
                win = ext_ref[rows, cols] if p == 0 else sh_ref[p - 1, rows, :]
                term = win * wdw_ref[w:w + 1, cols]
                acc = term if acc is None else acc + term
            acc_ref[pl.ds(r0, CONV_ROWS), cols] = acc
            return carry

        lax.fori_loop(0, tm // CONV_ROWS, row_block, 0)
    u = _silu(_layer_norm(acc_ref[...] + bdw_ref[...], cg_ref[...], cb_ref[...]))
    y = jnp.dot(u.astype(BF16), w_ref[...], preferred_element_type=F32) + b2_ref[...]
    z = ALPHA * x_ref[...] + mod_ref[5:6, :] * y
    o_ref[...] = _layer_norm(z, g_ref[1:2, :], b_ref[1:2, :])


def _conv(x, mods, layer, mod_row, u, w_dw, b_dw, cg, cb, w_pw2, b_pw2, j, ln_g, ln_b):
    bsz, t, d = x.shape
    tm = _row_tile(t)
    r = tm // CONV_HALO
    n_halo = t // CONV_HALO
    row = pl.BlockSpec((None, tm, d), lambda bi, i: (bi, i, 0))
    vec = pl.BlockSpec((None, 1, d), lambda bi, i: (j, 0, 0))
    ln = pl.BlockSpec((None, 3, d), lambda bi, i: (layer, 0, 0))
    return pl.pallas_call(
        _conv_kernel,
        out_shape=jax.ShapeDtypeStruct(x.shape, F32),
        grid=(bsz, t // tm),
        in_specs=[
            row,
            _mod_spec(layer, mod_row, 2),
            row,
            pl.BlockSpec((None, CONV_HALO, d), lambda bi, i: (bi, jnp.maximum(i * r - 1, 0), 0)),
            pl.BlockSpec((None, CONV_HALO, d), lambda bi, i: (bi, jnp.minimum((i + 1) * r, n_halo - 1), 0)),
            pl.BlockSpec((None, CONV_WIDTH, d), lambda bi, i: (j, 0, 0)),
            vec, vec, vec,
            pl.BlockSpec((None, d, d), lambda bi, i: (j, 0, 0)),
            vec, ln, ln,
        ],
        out_specs=row,
        scratch_shapes=[pltpu.VMEM((tm + 2 * CONV_HALO, d), F32),
                        pltpu.VMEM((7, tm + 8 * ((CONV_WIDTH + 7) // 8 - 1), HEAD_W), F32),
                        pltpu.VMEM((tm, d), F32)],
        compiler_params=_params("parallel", "parallel"),
        name="conv_mix",
    )(x, mods, u, u, u, w_dw, b_dw, cg, cb, w_pw2, b_pw2, ln_g, ln_b)


def kernel(x, c, ctx, c_ctx, w_mod, b_mod, ln_g, ln_b, ffn_w13, ffn_w2, ev_w_in, ev_w_out, da_lambda, da_norm,
           hg_lb, hg_norm, cv_w_pw1, cv_b_pw1, cv_w_dw, cv_b_dw, cv_ln_g, cv_ln_b, cv_w_pw2, cv_b_pw2):
    bsz, seq, d = x.shape
    per_layer_row = lambda a: a.reshape(a.shape[0], 1, a.shape[1])

    cc = jnp.zeros((16, d), F32).at[:bsz].set(c).at[bsz].set(c_ctx)
    mods = _mods(cc, w_mod, b_mod).reshape(DEPTH, 16, N_MOD, d)
    lat_row = lambda bi: bi
    ctx_row = lambda bi: bsz
    rope_tabs = _rope_tables(seq)
    lb_w = jax.nn.softmax(hg_lb.astype(F32), axis=0)
    lower_bounds = jnp.cumsum(lb_w, axis=0) - lb_w[:1]

    w13, w2 = ffn_w13.astype(BF16), ffn_w2.astype(BF16)
    w_in, w_out = ev_w_in.astype(BF16), ev_w_out.astype(BF16)
    w_pw1, w_pw2 = cv_w_pw1.astype(BF16), cv_w_pw2.astype(BF16)
    da_norm3, hg_norm3 = per_layer_row(da_norm), per_layer_row(hg_norm)
    conv_vecs = [per_layer_row(a) for a in (cv_b_dw, cv_ln_g, cv_ln_b)]
    b_pw1, b_pw2 = per_layer_row(cv_b_pw1), per_layer_row(cv_b_pw2)

    xl, xc = x, ctx
    for layer in range(DEPTH):
        j = layer // 2
        even = layer % 2 == 0
        ctx_pre = layer <= LAST_CTX_READ
        ctx_full = layer < LAST_CTX_READ
        ffn = lambda xx, mrow, fi, sub: _ffn(xx, mods, layer, mrow, w13, w2, fi, ln_g, ln_b, sub)

        xl = ffn(xl, lat_row, 0, 0)
        if ctx_pre:
            xc = ffn(xc, ctx_row, 0, 0)

        if even:
            lam_init = 0.8 - 0.6 * math.exp(-0.3 * layer)
            lv = da_lambda[j].astype(F32)
            lam = (jnp.exp(jnp.sum(lv[0] * lv[1])) - jnp.exp(jnp.sum(lv[2] * lv[3])) + lam_init).reshape(1)
            lb_f = lower_bounds[j, :GROUP_W].reshape(N_HEADS, HEAD_W)
            lb_b = lower_bounds[j, GROUP_W:].reshape(N_HEADS, HEAD_W)
            pb_l, pf_l = _inproj(xl, mods, layer, lat_row, w_in, j, rope_tabs)
            pb_c, pf_c = _inproj(xc, mods, layer, ctx_row, w_in, j, None)
            zeros = jnp.zeros((bsz, N_HEADS, HEAD_W, HEAD_W), F32)
            ocf, ocb, s_f, s_b = _hgrn_scan(pb_c, pf_c, lb_f, lb_b, zeros, zeros, ctx_full)
            a_l, olf, olb = _latent_mixer(lam, pb_l, pf_l, pb_c, lb_f, lb_b, s_f, s_b)
            merge = lambda xx, mrow, a, o_f, o_b, pb: _merge(
                xx, mods, layer, mrow, a, o_f, o_b, pb, da_norm3, hg_norm3, w_out, j, ln_g, ln_b, 1.0 - lam_init)
            if ctx_full:
                a_c = _attention(lam, pb_c, [pb_c])
                xc = merge(xc, ctx_row, a_c, ocf, ocb, pb_c)
            xl = merge(xl, lat_row, a_l, olf, olb, pb_l)
        else:
            conv = lambda xx, mrow, u: _conv(xx, mods, layer, mrow, u, cv_w_dw, *conv_vecs, w_pw2, b_pw2, j,
                                             ln_g, ln_b)
            if ctx_full:
                xc = conv(xc, ctx_row, _glu(xc, mods, layer, ctx_row, w_pw1, b_pw1, j))
            xl = conv(xl, lat_row, _glu(xl, mods, layer, lat_row, w_pw1, b_pw1, j))

        xl = ffn(xl, lat_row, 1, 2)
        if ctx_full:
            xc = ffn(xc, ctx_row, 1, 2)
    return xl
```

```python
import functools
import math

import jax
import jax.numpy as jnp
from jax import lax
from jax.experimental import pallas as pl
from jax.experimental.pallas import tpu as pltpu

F32 = jnp.float32
BF16 = jnp.bfloat16

D_MODEL = 1024
DEPTH = 4
GRID_W = 64
N_HEADS = 4
HEAD_W = 128
QK_DIM = 64
GROUP_W = N_HEADS * HEAD_W
D_FF = 2816
N_MOD = 9
CONV_WIDTH = 31
CONV_HALO = 16
ROPE_BASE = 10000.0
LAST_CTX_READ = 2 * ((DEPTH - 1) // 2)
ALPHA = (2 * DEPTH) ** 0.25
EPS = 1e-5
HG_CHUNK = 128
MXU_WIDTH = 256
FFN_SPLIT = (D_FF // MXU_WIDTH + 1) // 2 * MXU_WIDTH
FFN_CHUNKS = ((0, FFN_SPLIT), (FFN_SPLIT, D_FF))
ROW_TILE = 512
ATTN_ROW_TILE = 1024
ATTN_SUB_ROWS = 256
CONV_ROWS = 64
CONV_PARTIALS = 4
VMEM_LIMIT = 56 * 1024 * 1024

PB_WIDTH = 6 * GROUP_W
PF_WIDTH = 2 * GROUP_W
PB_Q, PB_K, PB_V, PB_HQ, PB_HI, PB_HG = range(6)

NT_DIMS = (((1,), (1,)), ((), ()))
TN_DIMS = (((0,), (0,)), ((), ()))


def _params(*sem):
    return pltpu.CompilerParams(dimension_semantics=sem, vmem_limit_bytes=VMEM_LIMIT)


def _layer_norm(z, g, b):
    mu = jnp.mean(z, axis=-1, keepdims=True)
    zc = z - mu
    var = jnp.mean(zc * zc, axis=-1, keepdims=True)
    return zc * lax.rsqrt(var + EPS) * g + b


def _rms_norm(z, g):
    return z * lax.rsqrt(jnp.mean(z * z, axis=-1, keepdims=True) + EPS) * g


def _silu(a):
    return a * jax.nn.sigmoid(a)


def _row_tile(t):
    return min(t, ROW_TILE)


def _mod_spec(layer, mod_row, grid_rank):
    if grid_rank == 2:
        return pl.BlockSpec((None, None, N_MOD, D_MODEL), lambda bi, i: (layer, mod_row(bi), 0, 0))
    return pl.BlockSpec((None, None, N_MOD, D_MODEL), lambda bi, i, j: (layer, mod_row(bi), 0, 0))


def _mods_kernel(c_ref, w_ref, b_ref, o_ref):
    a = _silu(c_ref[...]).astype(BF16)
    o_ref[...] = jnp.dot(a, w_ref[...].astype(BF16), preferred_element_type=F32) + b_ref[...]


def _mods(cc, w_mod, b_mod):
    n = cc.shape[0]
    tn = 1024
    return pl.pallas_call(
        _mods_kernel,
        out_shape=jax.ShapeDtypeStruct((DEPTH, n, N_MOD * D_MODEL), F32),
        grid=(DEPTH, N_MOD * D_MODEL // tn),
        in_specs=[
            pl.BlockSpec((n, D_MODEL), lambda l, j: (0, 0)),
            pl.BlockSpec((None, D_MODEL, tn), lambda l, j: (l, 0, j)),
            pl.BlockSpec((None, 1, tn), lambda l, j: (l, 0, j)),
        ],
        out_specs=pl.BlockSpec((None, n, tn), lambda l, j: (l, 0, j)),
        compiler_params=_params("parallel", "parallel"),
        name="mods",
    )(cc, w_mod, b_mod.reshape(DEPTH, 1, N_MOD * D_MODEL))


def _ffn_kernel(xn_ref, xp_ref, modn_ref, modp_ref, w13_ref, w2_ref, g_ref, b_ref, o_ref, h_scr, acc_scr,
                *, sub, n_tiles):
    s = pl.program_id(0)

    def prologue(dst):
        shift = modn_ref[3 * sub:3 * sub + 1, :]
        scale = modn_ref[3 * sub + 1:3 * sub + 2, :]
        h_scr[dst] = (xn_ref[...] * (1.0 + scale) + shift).astype(BF16)

    def epilogue(src):
        gate = modp_ref[3 * sub + 2:3 * sub + 3, :]
        z = ALPHA * xp_ref[...] + gate * (0.5 * acc_scr[src])
        o_ref[...] = _layer_norm(z, g_ref[sub:sub + 1, :], b_ref[sub:sub + 1, :])

    def matmuls(slot):
        h = h_scr[slot]
        ab = [(jnp.dot(h, w13_ref[:, lo:hi], preferred_element_type=F32),
               jnp.dot(h, w13_ref[:, D_FF + lo:D_FF + hi], preferred_element_type=F32)) for lo, hi in FFN_CHUNKS]
        ys = [jnp.dot((_silu(a) * b).astype(BF16), w2_ref[lo:hi, :], preferred_element_type=F32)
              for (a, b), (lo, hi) in zip(ab, FFN_CHUNKS)]
        acc_scr[slot] = functools.reduce(jnp.add, ys)

    @pl.when(s == 0)
    def _():
        prologue(0)
        acc_scr[1] = jnp.zeros(acc_scr.shape[1:], F32)

    for parity in range(2):
        @pl.when((s >= 1) & (s <= n_tiles) & (s % 2 == parity))
        def _():
            epilogue(parity)
            prologue(parity)
            matmuls(1 - parity)

    @pl.when(s == n_tiles + 1)
    def _():
        epilogue((n_tiles + 1) % 2)


def _ffn(x, mods, layer, mod_row, w13, w2, fi, ln_g, ln_b, sub):
    bsz, t, d = x.shape
    tm = _row_tile(t)
    per_b = t // tm
    n = bsz * per_b
    xf = x.reshape(bsz * t, d)
    nxt = lambda s: jnp.minimum(s, n - 1)
    prv = lambda s: jnp.clip(s - 2, 0, n - 1)
    resident = dict(pipeline_mode=pl.Buffered(1))
    out = pl.pallas_call(
        functools.partial(_ffn_kernel, sub=sub, n_tiles=n),
        out_shape=jax.ShapeDtypeStruct(xf.shape, F32),
        grid=(n + 2,),
        in_specs=[
            pl.BlockSpec((tm, d), lambda s: (nxt(s), 0)),
            pl.BlockSpec((tm, d), lambda s: (prv(s), 0)),
            pl.BlockSpec((None, None, N_MOD, d), lambda s: (layer, mod_row(nxt(s) // per_b), 0, 0)),
            pl.BlockSpec((None, None, N_MOD, d), lambda s: (layer, mod_row(prv(s) // per_b), 0, 0)),
            pl.BlockSpec((None, None, d, 2 * D_FF), lambda s: (layer, fi, 0, 0), **resident),
            pl.BlockSpec((None, None, D_FF, d), lambda s: (layer, fi, 0, 0), **resident),
            pl.BlockSpec((None, 3, d), lambda s: (layer, 0, 0)),
            pl.BlockSpec((None, 3, d), lambda s: (layer, 0, 0)),
        ],
        out_specs=pl.BlockSpec((tm, d), lambda s: (prv(s), 0)),
        scratch_shapes=[pltpu.VMEM((2, tm, d), BF16), pltpu.VMEM((2, tm, d), F32)],
        compiler_params=_params("arbitrary"),
        name="ffn",
    )(xf, xf, mods, mods, w13, w2, ln_g, ln_b)
    return out.reshape(x.shape)


def _inproj_kernel(*refs, rope):
    if rope:
        x_ref, mod_ref, w_ref, cos_ref, sa_ref, sb_ref, ob_ref, of_ref = refs
    else:
        x_ref, mod_ref, w_ref, ob_ref, of_ref = refs
    h = (x_ref[...] * (1.0 + mod_ref[4:5, :]) + mod_ref[3:4, :]).astype(BF16)
    w2 = 2 * GROUP_W
    proj = lambda grp: jnp.dot(h, w_ref[:, grp * w2:(grp + 1) * w2], preferred_element_type=F32)
    y = proj(0)
    if rope:
        wide = lambda r: jnp.concatenate([r[...]] * (w2 // HEAD_W), axis=1)
        y = (y * wide(cos_ref) + pltpu.roll(y, w2 - QK_DIM // 4, 1) * wide(sa_ref)
             + pltpu.roll(y, QK_DIM // 4, 1) * wide(sb_ref))
    ob_ref[:, 0:w2] = y.astype(BF16)
    ob_ref[:, w2:2 * w2] = proj(1).astype(BF16)
    of_ref[...] = proj(2)
    ob_ref[:, 2 * w2:3 * w2] = proj(3).astype(BF16)


def _inproj(x, mods, layer, mod_row, w_in, j, rope_tabs):
    bsz, t, d = x.shape
    tm = _row_tile(t)
    rope = rope_tabs is not None
    in_specs = [
        pl.BlockSpec((None, tm, d), lambda bi, i: (bi, i, 0)),
        _mod_spec(layer, mod_row, 2),
        pl.BlockSpec((None, d, 8 * GROUP_W), lambda bi, i: (j, 0, 0)),
    ]
    args = [x, mods, w_in]
    if rope:
        in_specs += [pl.BlockSpec((tm, HEAD_W), lambda bi, i: (i, 0))] * 3
        args += list(rope_tabs)
    return pl.pallas_call(
        functools.partial(_inproj_kernel, rope=rope),
        out_shape=[jax.ShapeDtypeStruct((bsz, t, PB_WIDTH), BF16), jax.ShapeDtypeStruct((bsz, t, PF_WIDTH), F32)],
        grid=(bsz, t // tm),
        in_specs=in_specs,
        out_specs=[pl.BlockSpec((None, tm, PB_WIDTH), lambda bi, i: (bi, i, 0)),
                   pl.BlockSpec((None, tm, PF_WIDTH), lambda bi, i: (bi, i, 0))],
        compiler_params=_params("parallel", "parallel"),
        name="inproj",
    )(*args)


def _rope_tables(n_tokens):
    pos = jnp.arange(n_tokens)
    half = QK_DIM // 2
    inv_freq = ROPE_BASE ** (-jnp.arange(0, half, 2, dtype=F32) / half)

    def table(p):
        ang = p.astype(F32)[:, None] * inv_freq[None, :]
        ang = jnp.concatenate([ang, ang], axis=-1)
        return jnp.cos(ang), jnp.sin(ang)

    (cos_r, sin_r), (cos_c, sin_c) = table(pos // GRID_W), table(pos % GRID_W)
    cos = jnp.concatenate([cos_r, cos_c, cos_r, cos_c], axis=-1)
    sin = jnp.concatenate([sin_r, sin_c, sin_r, sin_c], axis=-1)
    first = (jnp.arange(HEAD_W) % half) < half // 2
    return cos, jnp.where(first, -sin, 0.0), jnp.where(first, 0.0, sin)


def _emit_interleaved(*stage_lists):
    live = list(stage_lists)
    while live:
        for stages in list(live):
            if next(stages, StopIteration) is StopIteration:
                live.remove(stages)


def _attn_stages(lam_ref, q_ref, kv_refs, o_ref, n_sub):
    n_src = len(kv_refs) // 2
    rows = q_ref.shape[0] // n_sub
    lane = lax.broadcasted_iota(jnp.int32, (1, HEAD_W), 1)
    ks = [kv_refs[2 * i][...] for i in range(n_src)]
    vs = [jnp.concatenate([kv_refs[2 * i + 1][...], jnp.ones(kv_refs[2 * i + 1].shape, BF16)], axis=1)
          for i in range(n_src)]
    units = [(sub, comp) for sub in range(n_sub) for comp in range(2)]

    def scores(unit):
        sub, comp = unit
        q = q_ref[sub * rows:(sub + 1) * rows, :] * (QK_DIM ** -0.5)
        sel = (lane < QK_DIM) if comp == 0 else (lane >= QK_DIM)
        qc = jnp.where(sel, q, jnp.zeros_like(q))
        return [lax.dot_general(qc, k, NT_DIMS, preferred_element_type=F32) for k in ks]

    def softmax_numerators(ss):
        m = functools.reduce(jnp.maximum, [jnp.max(s, axis=-1, keepdims=True) for s in ss])
        return [jnp.exp((s - m).astype(BF16)) for s in ss]

    def mix(ps):
        acc = functools.reduce(jnp.add, [jnp.dot(p, v, preferred_element_type=F32) for p, v in zip(ps, vs)])
        return acc[:, :HEAD_W] / acc[:, HEAD_W:HEAD_W + 1]

    n = len(units)
    ss, ps, mixed = {}, {}, {}
    for t in range(n + 2):
        if t < n:
            ss[t] = scores(units[t])
        if 0 <= t - 2 < n:
            mixed[units[t - 2]] = mix(ps.pop(t - 2))
        if 0 <= t - 1 < n:
            ps[t - 1] = softmax_numerators(ss.pop(t - 1))
        yield
    lam = lam_ref[0]
    for sub in range(n_sub):
        o_ref[sub * rows:(sub + 1) * rows, :] = mixed[(sub, 0)] - lam * mixed[(sub, 1)]
    yield


def _attn_kernel(lam_ref, q_ref, *refs, n_sub):
    _emit_interleaved(_attn_stages(lam_ref, q_ref, refs[:-1], refs[-1], n_sub))


def _attention(lam, p_q, kv_sources):
    bsz, t, _ = p_q.shape
    tq = min(t, ATTN_ROW_TILE)
    n_src = len(kv_sources)
    in_specs = [
        pl.BlockSpec(memory_space=pltpu.SMEM),
        pl.BlockSpec((None, tq, HEAD_W), lambda bi, h, i: (bi, i, PB_Q * N_HEADS + h)),
    ]
    args = [lam, p_q]
    for src in kv_sources:
        tk = src.shape[1]
        in_specs.append(pl.BlockSpec((None, tk, HEAD_W), lambda bi, h, i: (bi, 0, PB_K * N_HEADS + h)))
        in_specs.append(pl.BlockSpec((None, tk, HEAD_W), lambda bi, h, i: (bi, 0, PB_V * N_HEADS + h)))
        args += [src, src]
    return pl.pallas_call(
        functools.partial(_attn_kernel, n_sub=max(tq // ATTN_SUB_ROWS, 1)),
        out_shape=jax.ShapeDtypeStruct((bsz, t, GROUP_W), F32),
        grid=(bsz, N_HEADS, t // tq),
        in_specs=in_specs,
        out_specs=pl.BlockSpec((None, tq, HEAD_W), lambda bi, h, i: (bi, i, h)),
        compiler_params=_params("parallel", "parallel", "arbitrary"),
        name="diff_attn",
    )(*args)


def _time_cumsum(x, tri):
    hi = x.astype(BF16)
    r1 = x - hi.astype(F32)
    mid = r1.astype(BF16)
    lo = (r1 - mid.astype(F32)).astype(BF16)
    parts = [jnp.dot(tri, part, preferred_element_type=F32) for part in (hi, mid, lo)]
    return parts[0] + (parts[1] + parts[2])


def _block_row(g, blk, idx):
    n, k = g.shape
    if blk >= 8:
        g4 = g.reshape(n // blk, blk // 8, 8, k)
        ref = g4[:, idx // 8:idx // 8 + 1, idx % 8:idx % 8 + 1, :]
        return jnp.broadcast_to(ref, g4.shape).reshape(n, k)
    g3 = g.reshape(n // 8, 8, k)
    sub = lax.broadcasted_iota(jnp.int32, (1, 8, 1), 1) // blk
    out = None
    for j in range(8 // blk):
        ref = jnp.broadcast_to(g3[:, j * blk + idx:j * blk + idx + 1, :], g3.shape)
        out = ref if out is None else jnp.where(sub == j, ref, out)
    return out.reshape(n, k)


def _hgrn_refs(refs, with_output):
    ins = [refs[0:5], refs[5:10]]
    outs = refs[10:]
    o_refs, s_refs = (outs[0:2], outs[2:4]) if with_output else (None, outs[0:2])
    return ins, o_refs, s_refs


def _hgrn_load_state(ins, s_refs, chunk_axis):
    @pl.when(pl.program_id(chunk_axis) == 0)
    def _():
        for d in range(2):
            s_refs[d][...] = ins[d][4][...]


def _hgrn_kernel(*refs, with_output):
    ins, o_refs, s_refs = _hgrn_refs(refs, with_output)
    _hgrn_load_state(ins, s_refs, 1)
    _emit_interleaved(_hgrn_stages(ins, o_refs, s_refs))


def _hgrn_stages(ins, o_refs, s_refs):
    with_output = o_refs is not None
    n = ins[0][1].shape[0]
    n_levels = n.bit_length() - 1
    rt = lax.broadcasted_iota(jnp.int32, (n, n), 0)
    cs = lax.broadcasted_iota(jnp.int32, (n, n), 1)
    tri = [(cs <= rt).astype(BF16), (cs >= rt).astype(BF16)]
    last = [n - 1, 0]
    if with_output:
        differ = rt ^ cs
        top_bit = functools.reduce(jnp.add, [(differ >= (1 << b)).astype(jnp.int32) for b in range(1, n_levels)])
        level = [jnp.where(earlier, top_bit + 1, jnp.where(rt == cs, 0, -1)) for earlier in (cs < rt, cs > rt)]
    chains = [(d, h) for d in range(2) for h in range(N_HEADS)]
    lanes = [slice(h * HEAD_W, (h + 1) * HEAD_W) for h in range(N_HEADS)]
    kk, g = {}, {}
    for d, h in chains:
        lb = ins[d][3][h:h + 1, :]
        f = lb + (1.0 - lb) * jax.nn.sigmoid(ins[d][1][:, lanes[h]])
        kk[d, h] = 1.0 - f
        g[d, h] = _time_cumsum(jnp.log2(f), tri[d])
    yield
    if with_output:
        qq = {(d, h): _silu(ins[d][0][:, lanes[h]].astype(F32)) for d, h in chains}
        qb = {c: qq[c].astype(BF16) for c in chains}
        kb = {c: kk[c].astype(BF16) for c in chains}
        att = {(d, h): jnp.where(level[d] == 0,
                                 lax.dot_general(qb[d, h], kb[d, h], NT_DIMS, preferred_element_type=F32), 0.0)
               for d, h in chains}
        for lv in range(1, n_levels + 1):
            half = 1 << (lv - 1)
            for d, h in chains:
                boundary = _block_row(g[d, h], 2 * half, half if d == 1 else half - 1)
                e = jnp.exp2(-jnp.abs(g[d, h] - boundary)).astype(BF16)
                p = lax.dot_general(qb[d, h] * e, kb[d, h] * e, NT_DIMS, preferred_element_type=F32)
                att[d, h] = jnp.where(level[d] == lv, p, att[d, h])
            yield
        for d, h in chains:
            o = jnp.dot(att[d, h].astype(BF16), ins[d][2][:, lanes[h]], preferred_element_type=F32)
            o = o + lax.dot_general((qq[d, h] * jnp.exp2(g[d, h])).astype(BF16), s_refs[d][h].astype(BF16),
                                    NT_DIMS, preferred_element_type=F32)
            o_refs[d][:, lanes[h]] = o
        yield
    for d, h in chains:
        g_last = g[d, h][last[d]:last[d] + 1, :]
        kx = (kk[d, h] * jnp.exp2(g_last - g[d, h])).astype(BF16)
        s_refs[d][h] = jnp.exp2(g_last) * s_refs[d][h] + lax.dot_general(
            ins[d][2][:, lanes[h]], kx, TN_DIMS, preferred_element_type=F32)


def _hgrn_operands(p_b, p_f, lb_f, lb_b, s0_f, s0_b, with_output):
    bsz, t, _ = p_b.shape
    c = HG_CHUNK
    nc = t // c
    cmaps = (lambda ci: ci, lambda ci: nc - 1 - ci)
    col = lambda d, blk: (lambda bi, ci: (bi, cmaps[d](ci), blk))
    state_spec = pl.BlockSpec((None, N_HEADS, HEAD_W, HEAD_W), lambda bi, ci: (bi, 0, 0, 0))
    state_shape = jax.ShapeDtypeStruct((bsz, N_HEADS, HEAD_W, HEAD_W), F32)
    in_specs, args = [], []
    for d, (lb, s0) in enumerate(((lb_f, s0_f), (lb_b, s0_b))):
        in_specs += [
            pl.BlockSpec((None, c, GROUP_W), col(d, PB_HQ)),
            pl.BlockSpec((None, c, GROUP_W), col(d, d)),
            pl.BlockSpec((None, c, GROUP_W), col(d, PB_HI)),
            pl.BlockSpec((N_HEADS, HEAD_W), lambda bi, ci: (0, 0)),
            state_spec,
        ]
        args += [p_b, p_f, p_b, lb, s0]
    out_shape, out_specs = [state_shape, state_shape], [state_spec, state_spec]
    if with_output:
        out_shape = [jax.ShapeDtypeStruct((bsz, t, GROUP_W), F32)] * 2 + out_shape
        out_specs = [pl.BlockSpec((None, c, GROUP_W), col(d, 0)) for d in range(2)] + out_specs
    return in_specs, args, out_shape, out_specs


def _hgrn_scan(p_b, p_f, lb_f, lb_b, s0_f, s0_b, with_output):
    in_specs, args, out_shape, out_specs = _hgrn_operands(p_b, p_f, lb_f, lb_b, s0_f, s0_b, with_output)
    res = pl.pallas_call(
        functools.partial(_hgrn_kernel, with_output=with_output),
        out_shape=out_shape,
        grid=(p_b.shape[0], p_b.shape[1] // HG_CHUNK),
        in_specs=in_specs,
        out_specs=out_specs,
        compiler_params=_params("parallel", "arbitrary"),
        name="hgrn",
    )(*args)
    return tuple(res) if with_output else (None, None, res[0], res[1])


def _mixer_kernel(*refs, n_attn_in, n_sub):
    attn_in, hgrn_in, outs = refs[:n_attn_in], refs[n_attn_in:n_attn_in + 10], refs[n_attn_in + 10:]
    ins, o_refs, s_refs = _hgrn_refs(hgrn_in + outs[1:], True)
    _hgrn_load_state(ins, s_refs, 1)
    _emit_interleaved(_attn_stages(attn_in[0], attn_in[1], attn_in[2:], outs[0], n_sub),
                      _hgrn_stages(ins, o_refs, s_refs))


def _latent_mixer(lam, pb_l, pf_l, pb_c, lb_f, lb_b, s0_f, s0_b):
    bsz, t, _ = pb_l.shape
    nc = t // HG_CHUNK
    tq = t * N_HEADS // nc
    nq = t // tq
    head_col = lambda grp: (lambda bi, ci: (bi, 0, grp * N_HEADS + ci // nq))
    in_specs = [
        pl.BlockSpec(memory_space=pltpu.SMEM),
        pl.BlockSpec((None, tq, HEAD_W), lambda bi, ci: (bi, ci % nq, PB_Q * N_HEADS + ci // nq)),
    ]
    args = [lam, pb_l]
    for src in (pb_l, pb_c):
        in_specs += [pl.BlockSpec((None, src.shape[1], HEAD_W), head_col(PB_K)),
                     pl.BlockSpec((None, src.shape[1], HEAD_W), head_col(PB_V))]
        args += [src, src]
    n_attn_in = len(args)
    h_specs, h_args, h_shape, h_out_specs = _hgrn_operands(pb_l, pf_l, lb_f, lb_b, s0_f, s0_b, True)
    res = pl.pallas_call(
        functools.partial(_mixer_kernel, n_attn_in=n_attn_in, n_sub=max(tq // ATTN_SUB_ROWS, 1)),
        out_shape=[jax.ShapeDtypeStruct((bsz, t, GROUP_W), F32)] + h_shape,
        grid=(bsz, nc),
        in_specs=in_specs + h_specs,
        out_specs=[pl.BlockSpec((None, tq, HEAD_W), lambda bi, ci: (bi, ci % nq, ci // nq))] + h_out_specs,
        compiler_params=_params("parallel", "arbitrary"),
        name="latent_mixer",
    )(*args, *h_args)
    return res[0], res[1], res[2]


def _merge_kernel(x_ref, mod_ref, a_ref, of_ref, ob_ref, gt_ref, dn_ref, hn_ref, w_ref, g_ref, b_ref, o_ref,
                  *, da_scale):
    a = a_ref[...]
    o = of_ref[...] + ob_ref[...]
    parts_a, parts_o = [], []
    for h in range(N_HEADS):
        sl = slice(h * HEAD_W, (h + 1) * HEAD_W)
        parts_a.append(_rms_norm(a[:, sl], dn_ref[:, sl]) * da_scale)
        parts_o.append(_rms_norm(o[:, sl], hn_ref[:, sl]))
    ya = jnp.concatenate(parts_a, axis=1)
    yo = jnp.concatenate(parts_o, axis=1) * _silu(gt_ref[...].astype(F32))
    y = jnp.dot(jnp.concatenate([ya, yo], axis=1).astype(BF16), w_ref[...], preferred_element_type=F32)
    z = ALPHA * x_ref[...] + mod_ref[5:6, :] * y
    o_ref[...] = _layer_norm(z, g_ref[1:2, :], b_ref[1:2, :])


def _merge(x, mods, layer, mod_row, attn, o_f, o_b, p_b, da_norm, hg_norm, w_out, j, ln_g, ln_b, da_scale):
    bsz, t, d = x.shape
    tm = _row_tile(t)
    grp = pl.BlockSpec((None, tm, GROUP_W), lambda bi, i: (bi, i, 0))
    vec = lambda n: pl.BlockSpec((None, 1, n), lambda bi, i: (j, 0, 0))
    ln = pl.BlockSpec((None, 3, d), lambda bi, i: (layer, 0, 0))
    return pl.pallas_call(
        functools.partial(_merge_kernel, da_scale=da_scale),
        out_shape=jax.ShapeDtypeStruct(x.shape, F32),
        grid=(bsz, t // tm),
        in_specs=[
            pl.BlockSpec((None, tm, d), lambda bi, i: (bi, i, 0)),
            _mod_spec(layer, mod_row, 2),
            grp, grp, grp,
            pl.BlockSpec((None, tm, GROUP_W), lambda bi, i: (bi, i, PB_HG)),
            vec(GROUP_W), vec(GROUP_W),
            pl.BlockSpec((None, d, d), lambda bi, i: (j, 0, 0)),
            ln, ln,
        ],
        out_specs=pl.BlockSpec((None, tm, d), lambda bi, i: (bi, i, 0)),
        compiler_params=_params("parallel", "parallel"),
        name="even_merge",
    )(x, mods, attn, o_f, o_b, p_b, da_norm, hg_norm, w_out, ln_g, ln_b)


def _glu_kernel(x_ref, mod_ref, w_ref, b_ref, o_ref):
    h = (x_ref[...] * (1.0 + mod_ref[4:5, :]) + mod_ref[3:4, :]).astype(BF16)
    y = jnp.dot(h, w_ref[...], preferred_element_type=F32) + b_ref[...]
    d = o_ref.shape[-1]
    o_ref[...] = y[:, :d] * jax.nn.sigmoid(y[:, d:])


def _glu(x, mods, layer, mod_row, w_pw1, b_pw1, j):
    bsz, t, d = x.shape
    tm = _row_tile(t)
    return pl.pallas_call(
        _glu_kernel,
        out_shape=jax.ShapeDtypeStruct(x.shape, F32),
        grid=(bsz, t // tm),
        in_specs=[
            pl.BlockSpec((None, tm, d), lambda bi, i: (bi, i, 0)),
            _mod_spec(layer, mod_row, 2),
            pl.BlockSpec((None, d, 2 * d), lambda bi, i: (j, 0, 0)),
            pl.BlockSpec((None, 1, 2 * d), lambda bi, i: (j, 0, 0)),
        ],
        out_specs=pl.BlockSpec((None, tm, d), lambda bi, i: (bi, i, 0)),
        compiler_params=_params("parallel", "parallel"),
        name="conv_glu",
    )(x, mods, w_pw1, b_pw1)


def _conv_kernel(x_ref, mod_ref, u_ref, up_ref, un_ref, wdw_ref, bdw_ref, cg_ref, cb_ref, w_ref, b2_ref,
                 g_ref, b_ref, o_ref, ext_ref, sh_ref, acc_ref):
    i = pl.program_id(1)
    tm = u_ref.shape[0]
    ext_ref[0:CONV_HALO, :] = jnp.where(i > 0, up_ref[...], 0.0)
    ext_ref[CONV_HALO:CONV_HALO + tm, :] = u_ref[...]
    ext_ref[CONV_HALO + tm:, :] = jnp.where(i < pl.num_programs(1) - 1, un_ref[...], 0.0)
    off = CONV_HALO - CONV_WIDTH // 2
    span = sh_ref.shape[1]
    for cb in range(u_ref.shape[1] // HEAD_W):
        cols = slice(cb * HEAD_W, (cb + 1) * HEAD_W)
        for p in range(1, 8):
            sh_ref[p - 1] = ext_ref[p:p + span, cols]

        def row_block(rb, carry):
            r0 = pl.multiple_of(rb * CONV_ROWS, CONV_ROWS)
            partial = [None] * CONV_PARTIALS
            for w in range(CONV_WIDTH):
                p, a = (off + w) % 8, (off + w) // 8
                rows = pl.ds(r0 + 8 * a, CONV_ROWS)
                win = ext_ref[rows, cols] if p == 0 else sh_ref[p - 1, rows, :]
                term = win * wdw_ref[w:w + 1, cols]
                k = w % CONV_PARTIALS
                partial[k] = term if partial[k] is None else partial[k] + term
            while len(partial) > 1:
                partial = [a + b for a, b in zip(partial[0::2], partial[1::2])]
            acc_ref[pl.ds(r0, CONV_ROWS), cols] = partial[0]
            return carry

        lax.fori_loop(0, tm // CONV_ROWS, row_block, 0)
    u = _silu(_layer_norm(acc_ref[...] + bdw_ref[...], cg_ref[...], cb_ref[...]))
    y = jnp.dot(u.astype(BF16), w_ref[...], preferred_element_type=F32) + b2_ref[...]
    z = ALPHA * x_ref[...] + mod_ref[5:6, :] * y
    o_ref[...] = _layer_norm(z, g_ref[1:2, :], b_ref[1:2, :])


def _conv(x, mods, layer, mod_row, u, w_dw, b_dw, cg, cb, w_pw2, b_pw2, j, ln_g, ln_b):
    bsz, t, d = x.shape
    tm = _row_tile(t)
    r = tm // CONV_HALO
    n_halo = t // CONV_HALO
    row = pl.BlockSpec((None, tm, d), lambda bi, i: (bi, i, 0))
    vec = pl.BlockSpec((None, 1, d), lambda bi, i: (j, 0, 0))
    ln = pl.BlockSpec((None, 3, d), lambda bi, i: (layer, 0, 0))
    return pl.pallas_call(
        _conv_kernel,
        out_shape=jax.ShapeDtypeStruct(x.shape, F32),
        grid=(bsz, t // tm),
        in_specs=[
            row,
            _mod_spec(layer, mod_row, 2),
            row,
            pl.BlockSpec((None, CONV_HALO, d), lambda bi, i: (bi, jnp.maximum(i * r - 1, 0), 0)),
            pl.BlockSpec((None, CONV_HALO, d), lambda bi, i: (bi, jnp.minimum((i + 1) * r, n_halo - 1), 0)),
            pl.BlockSpec((None, CONV_WIDTH, d), lambda bi, i: (j, 0, 0)),
            vec, vec, vec,
            pl.BlockSpec((None, d, d), lambda bi, i: (j, 0, 0)),
            vec, ln, ln,
        ],
        out_specs=row,
        scratch_shapes=[pltpu.VMEM((tm + 2 * CONV_HALO, d), F32),
                        pltpu.VMEM((7, tm + 8 * ((CONV_WIDTH + 7) // 8 - 1), HEAD_W), F32),
                        pltpu.VMEM((tm, d), F32)],
        compiler_params=_params("parallel", "parallel"),
        name="conv_mix",
    )(x, mods, u, u, u, w_dw, b_dw, cg, cb, w_pw2, b_pw2, ln_g, ln_b)


def kernel(x, c, ctx, c_ctx, w_mod, b_mod, ln_g, ln_b, ffn_w13, ffn_w2, ev_w_in, ev_w_out, da_lambda, da_norm,
           hg_lb, hg_norm, cv_w_pw1, cv_b_pw1, cv_w_dw, cv_b_dw, cv_ln_g, cv_ln_b, cv_w_pw2, cv_b_pw2):
    bsz, seq, d = x.shape
    per_layer_row = lambda a: a.reshape(a.shape[0], 1, a.shape[1])

    cc = jnp.zeros((16, d), F32).at[:bsz].set(c).at[bsz].set(c_ctx)
    mods = _mods(cc, w_mod, b_mod).reshape(DEPTH, 16, N_MOD, d)
    lat_row = lambda bi: bi
    ctx_row = lambda bi: bsz
    rope_tabs = _rope_tables(seq)
    lb_w = jax.nn.softmax(hg_lb.astype(F32), axis=0)
    lower_bounds = jnp.cumsum(lb_w, axis=0) - lb_w[:1]

    w13, w2 = ffn_w13.astype(BF16), ffn_w2.astype(BF16)
    w_in, w_out = ev_w_in.astype(BF16), ev_w_out.astype(BF16)
    w_pw1, w_pw2 = cv_w_pw1.astype(BF16), cv_w_pw2.astype(BF16)
    da_norm3, hg_norm3 = per_layer_row(da_norm), per_layer_row(hg_norm)
    conv_vecs = [per_layer_row(a) for a in (cv_b_dw, cv_ln_g, cv_ln_b)]
    b_pw1, b_pw2 = per_layer_row(cv_b_pw1), per_layer_row(cv_b_pw2)

    xl, xc = x, ctx
    for layer in range(DEPTH):
        j = layer // 2
        even = layer % 2 == 0
        ctx_pre = layer <= LAST_CTX_READ
        ctx_full = layer < LAST_CTX_READ
        ffn = lambda xx, mrow, fi, sub: _ffn(xx, mods, layer, mrow, w13, w2, fi, ln_g, ln_b, sub)

        xl = ffn(xl, lat_row, 0, 0)
        if ctx_pre:
            xc = ffn(xc, ctx_row, 0, 0)

        if even:
            lam_init = 0.8 - 0.6 * math.exp(-0.3 * layer)
            lv = da_lambda[j].astype(F32)
            lam = (jnp.exp(jnp.sum(lv[0] * lv[1])) - jnp.exp(jnp.sum(lv[2] * lv[3])) + lam_init).reshape(1)
            lb_f = lower_bounds[j, :GROUP_W].reshape(N_HEADS, HEAD_W)
            lb_b = lower_bounds[j, GROUP_W:].reshape(N_HEADS, HEAD_W)
            pb_l, pf_l = _inproj(xl, mods, layer, lat_row, w_in, j, rope_tabs)
            pb_c, pf_c = _inproj(xc, mods, layer, ctx_row, w_in, j, None)
            zeros = jnp.zeros((bsz, N_HEADS, HEAD_W, HEAD_W), F32)
            ocf, ocb, s_f, s_b = _hgrn_scan(pb_c, pf_c, lb_f, lb_b, zeros, zeros, ctx_full)
            a_l, olf, olb = _latent_mixer(lam, pb_l, pf_l, pb_c, lb_f, lb_b, s_f, s_b)
            merge = lambda xx, mrow, a, o_f, o_b, pb: _merge(
                xx, mods, layer, mrow, a, o_f, o_b, pb, da_norm3, hg_norm3, w_out, j, ln_g, ln_b, 1.0 - lam_init)
            if ctx_full:
                a_c = _attention(lam, pb_c, [pb_c])
                xc = merge(xc, ctx_row, a_c, ocf, ocb, pb_c)
            xl = merge(xl, lat_row, a_l, olf, olb, pb_l)
        else:
            conv = lambda xx, mrow, u: _conv(xx, mods, layer, mrow, u, cv_w_dw, *conv_vecs, w_pw2, b_pw2, j,
                                             ln_g, ln_b)
            if ctx_full:
                xc = conv(xc, ctx_row, _glu(xc, mods, layer, ctx_row, w_pw1, b_pw1, j))
            xl = conv(xl, lat_row, _glu(xl, mods, layer, lat_row, w_pw1, b_pw1, j))

        xl = ffn(xl, lat_row, 1, 2)
        if ctx_full:
            xc = ffn(xc, ctx_row, 1, 2)
    return xl
```

```python
import functools
import math

import jax
import jax.numpy as jnp
from jax import lax
from jax.experimental import pallas as pl
from jax.experimental.pallas import tpu as pltpu

F32 = jnp.float32
BF16 = jnp.bfloat16

D_MODEL = 1024
DEPTH = 4
GRID_W = 64
N_HEADS = 4
HEAD_W = 128
QK_DIM = 64
GROUP_W = N_HEADS * HEAD_W
D_FF = 2816
N_MOD = 9
CONV_WIDTH = 31
CONV_HALO = 16
ROPE_BASE = 10000.0
LAST_CTX_READ = 2 * ((DEPTH - 1) // 2)
ALPHA = (2 * DEPTH) ** 0.25
EPS = 1e-5
HG_CHUNK = 128
MIXER_CHUNKS = 4
MXU_WIDTH = 256
FFN_SPLIT = (D_FF // MXU_WIDTH + 1) // 2 * MXU_WIDTH
FFN_CHUNKS = ((0, FFN_SPLIT), (FFN_SPLIT, D_FF))
ROW_TILE = 512
ATTN_ROW_TILE = 1024
ATTN_SUB_ROWS = 256
CONV_ROWS = 64
CONV_PARTIALS = 4
VMEM_LIMIT = 56 * 1024 * 1024

PB_WIDTH = 6 * GROUP_W
PF_WIDTH = 2 * GROUP_W
PB_Q, PB_K, PB_V, PB_HQ, PB_HI, PB_HG = range(6)

NT_DIMS = (((1,), (1,)), ((), ()))
TN_DIMS = (((0,), (0,)), ((), ()))


def _params(*sem):
    return pltpu.CompilerParams(dimension_semantics=sem, vmem_limit_bytes=VMEM_LIMIT)


def _layer_norm(z, g, b):
    mu = jnp.mean(z, axis=-1, keepdims=True)
    zc = z - mu
    var = jnp.mean(zc * zc, axis=-1, keepdims=True)
    return zc * lax.rsqrt(var + EPS) * g + b


def _rms_norm(z, g):
    return z * lax.rsqrt(jnp.mean(z * z, axis=-1, keepdims=True) + EPS) * g


def _silu(a):
    return a * jax.nn.sigmoid(a)


def _row_tile(t):
    return min(t, ROW_TILE)


def _mod_spec(layer, mod_row, grid_rank):
    if grid_rank == 2:
        return pl.BlockSpec((None, None, N_MOD, D_MODEL), lambda bi, i: (layer, mod_row(bi), 0, 0))
    return pl.BlockSpec((None, None, N_MOD, D_MODEL), lambda bi, i, j: (layer, mod_row(bi), 0, 0))


def _mods_kernel(c_ref, w_ref, b_ref, o_ref):
    a = _silu(c_ref[...]).astype(BF16)
    o_ref[...] = jnp.dot(a, w_ref[...].astype(BF16), preferred_element_type=F32) + b_ref[...]


def _mods(cc, w_mod, b_mod):
    n = cc.shape[0]
    tn = 1024
    return pl.pallas_call(
        _mods_kernel,
        out_shape=jax.ShapeDtypeStruct((DEPTH, n, N_MOD * D_MODEL), F32),
        grid=(DEPTH, N_MOD * D_MODEL // tn),
        in_specs=[
            pl.BlockSpec((n, D_MODEL), lambda l, j: (0, 0)),
            pl.BlockSpec((None, D_MODEL, tn), lambda l, j: (l, 0, j)),
            pl.BlockSpec((None, 1, tn), lambda l, j: (l, 0, j)),
        ],
        out_specs=pl.BlockSpec((None, n, tn), lambda l, j: (l, 0, j)),
        compiler_params=_params("parallel", "parallel"),
        name="mods",
    )(cc, w_mod, b_mod.reshape(DEPTH, 1, N_MOD * D_MODEL))


def _ffn_kernel(xn_ref, xp_ref, modn_ref, modp_ref, w13_ref, w2_ref, g_ref, b_ref, o_ref, h_scr, acc_scr,
                *, sub, n_tiles):
    s = pl.program_id(0)

    def prologue(dst):
        shift = modn_ref[3 * sub:3 * sub + 1, :]
        scale = modn_ref[3 * sub + 1:3 * sub + 2, :]
        h_scr[dst] = (xn_ref[...] * (1.0 + scale) + shift).astype(BF16)

    def epilogue(src):
        gate = modp_ref[3 * sub + 2:3 * sub + 3, :]
        z = ALPHA * xp_ref[...] + gate * (0.5 * acc_scr[src])
        o_ref[...] = _layer_norm(z, g_ref[sub:sub + 1, :], b_ref[sub:sub + 1, :])

    def matmuls(slot):
        h = h_scr[slot]
        ab = [(jnp.dot(h, w13_ref[:, lo:hi], preferred_element_type=F32),
               jnp.dot(h, w13_ref[:, D_FF + lo:D_FF + hi], preferred_element_type=F32)) for lo, hi in FFN_CHUNKS]
        ys = [jnp.dot((_silu(a) * b).astype(BF16), w2_ref[lo:hi, :], preferred_element_type=F32)
              for (a, b), (lo, hi) in zip(ab, FFN_CHUNKS)]
        acc_scr[slot] = functools.reduce(jnp.add, ys)

    @pl.when(s == 0)
    def _():
        prologue(0)
        acc_scr[1] = jnp.zeros(acc_scr.shape[1:], F32)

    for parity in range(2):
        @pl.when((s >= 1) & (s <= n_tiles) & (s % 2 == parity))
        def _():
            epilogue(parity)
            prologue(parity)
            matmuls(1 - parity)

    @pl.when(s == n_tiles + 1)
    def _():
        epilogue((n_tiles + 1) % 2)


def _ffn(x, mods, layer, mod_row, w13, w2, fi, ln_g, ln_b, sub):
    bsz, t, d = x.shape
    tm = _row_tile(t)
    per_b = t // tm
    n = bsz * per_b
    xf = x.reshape(bsz * t, d)
    nxt = lambda s: jnp.minimum(s, n - 1)
    prv = lambda s: jnp.clip(s - 2, 0, n - 1)
    resident = dict(pipeline_mode=pl.Buffered(1))
    out = pl.pallas_call(
        functools.partial(_ffn_kernel, sub=sub, n_tiles=n),
        out_shape=jax.ShapeDtypeStruct(xf.shape, F32),
        grid=(n + 2,),
        in_specs=[
            pl.BlockSpec((tm, d), lambda s: (nxt(s), 0)),
            pl.BlockSpec((tm, d), lambda s: (prv(s), 0)),
            pl.BlockSpec((None, None, N_MOD, d), lambda s: (layer, mod_row(nxt(s) // per_b), 0, 0)),
            pl.BlockSpec((None, None, N_MOD, d), lambda s: (layer, mod_row(prv(s) // per_b), 0, 0)),
            pl.BlockSpec((None, None, d, 2 * D_FF), lambda s: (layer, fi, 0, 0), **resident),
            pl.BlockSpec((None, None, D_FF, d), lambda s: (layer, fi, 0, 0), **resident),
            pl.BlockSpec((None, 3, d), lambda s: (layer, 0, 0)),
            pl.BlockSpec((None, 3, d), lambda s: (layer, 0, 0)),
        ],
        out_specs=pl.BlockSpec((tm, d), lambda s: (prv(s), 0)),
        scratch_shapes=[pltpu.VMEM((2, tm, d), BF16), pltpu.VMEM((2, tm, d), F32)],
        compiler_params=_params("arbitrary"),
        name="ffn",
    )(xf, xf, mods, mods, w13, w2, ln_g, ln_b)
    return out.reshape(x.shape)


def _inproj_kernel(*refs, rope):
    if rope:
        x_ref, mod_ref, w_ref, cos_ref, sa_ref, sb_ref, ob_ref, of_ref = refs
    else:
        x_ref, mod_ref, w_ref, ob_ref, of_ref = refs
    h = (x_ref[...] * (1.0 + mod_ref[4:5, :]) + mod_ref[3:4, :]).astype(BF16)
    w2 = 2 * GROUP_W
    proj = lambda grp: jnp.dot(h, w_ref[:, grp * w2:(grp + 1) * w2], preferred_element_type=F32)
    y = proj(0)
    if rope:
        wide = lambda r: jnp.concatenate([r[...]] * (w2 // HEAD_W), axis=1)
        y = (y * wide(cos_ref) + pltpu.roll(y, w2 - QK_DIM // 4, 1) * wide(sa_ref)
             + pltpu.roll(y, QK_DIM // 4, 1) * wide(sb_ref))
    ob_ref[:, 0:w2] = y.astype(BF16)
    ob_ref[:, w2:2 * w2] = proj(1).astype(BF16)
    of_ref[...] = proj(2)
    ob_ref[:, 2 * w2:3 * w2] = proj(3).astype(BF16)


def _inproj(x, mods, layer, mod_row, w_in, j, rope_tabs):
    bsz, t, d = x.shape
    tm = _row_tile(t)
    rope = rope_tabs is not None
    in_specs = [
        pl.BlockSpec((None, tm, d), lambda bi, i: (bi, i, 0)),
        _mod_spec(layer, mod_row, 2),
        pl.BlockSpec((None, d, 8 * GROUP_W), lambda bi, i: (j, 0, 0)),
    ]
    args = [x, mods, w_in]
    if rope:
        in_specs += [pl.BlockSpec((tm, HEAD_W), lambda bi, i: (i, 0))] * 3
        args += list(rope_tabs)
    return pl.pallas_call(
        functools.partial(_inproj_kernel, rope=rope),
        out_shape=[jax.ShapeDtypeStruct((bsz, t, PB_WIDTH), BF16), jax.ShapeDtypeStruct((bsz, t, PF_WIDTH), F32)],
        grid=(bsz, t // tm),
        in_specs=in_specs,
        out_specs=[pl.BlockSpec((None, tm, PB_WIDTH), lambda bi, i: (bi, i, 0)),
                   pl.BlockSpec((None, tm, PF_WIDTH), lambda bi, i: (bi, i, 0))],
        compiler_params=_params("parallel", "parallel"),
        name="inproj",
    )(*args)


def _rope_tables(n_tokens):
    pos = jnp.arange(n_tokens)
    half = QK_DIM // 2
    inv_freq = ROPE_BASE ** (-jnp.arange(0, half, 2, dtype=F32) / half)

    def table(p):
        ang = p.astype(F32)[:, None] * inv_freq[None, :]
        ang = jnp.concatenate([ang, ang], axis=-1)
        return jnp.cos(ang), jnp.sin(ang)

    (cos_r, sin_r), (cos_c, sin_c) = table(pos // GRID_W), table(pos % GRID_W)
    cos = jnp.concatenate([cos_r, cos_c, cos_r, cos_c], axis=-1)
    sin = jnp.concatenate([sin_r, sin_c, sin_r, sin_c], axis=-1)
    first = (jnp.arange(HEAD_W) % half) < half // 2
    return cos, jnp.where(first, -sin, 0.0), jnp.where(first, 0.0, sin)


def _emit_interleaved(*stage_lists):
    live = list(stage_lists)
    while live:
        for stages in list(live):
            if next(stages, StopIteration) is StopIteration:
                live.remove(stages)


def _attn_stages(lam_ref, q_ref, kv_refs, o_ref, n_sub, tile=0, n_tiles=1):
    n_src = len(kv_refs) // 2
    rows = q_ref.shape[0] // (n_tiles * n_sub)
    first = tile * n_sub
    lane = lax.broadcasted_iota(jnp.int32, (1, HEAD_W), 1)
    ks = [kv_refs[2 * i][...] for i in range(n_src)]
    vs = [jnp.concatenate([kv_refs[2 * i + 1][...], jnp.ones(kv_refs[2 * i + 1].shape, BF16)], axis=1)
          for i in range(n_src)]
    units = [(sub, comp) for sub in range(n_sub) for comp in range(2)]

    def scores(unit):
        sub, comp = unit
        q = q_ref[(first + sub) * rows:(first + sub + 1) * rows, :] * (QK_DIM ** -0.5)
        sel = (lane < QK_DIM) if comp == 0 else (lane >= QK_DIM)
        qc = jnp.where(sel, q, jnp.zeros_like(q))
        return [lax.dot_general(qc, k, NT_DIMS, preferred_element_type=F32) for k in ks]

    def softmax_numerators(ss):
        m = functools.reduce(jnp.maximum, [jnp.max(s, axis=-1, keepdims=True) for s in ss])
        return [jnp.exp((s - m).astype(BF16)) for s in ss]

    def mix(ps):
        acc = functools.reduce(jnp.add, [jnp.dot(p, v, preferred_element_type=F32) for p, v in zip(ps, vs)])
        return acc[:, :HEAD_W] / acc[:, HEAD_W:HEAD_W + 1]

    n = len(units)
    ss, ps, mixed = {}, {}, {}
    for t in range(n + 2):
        if t < n:
            ss[t] = scores(units[t])
        if 0 <= t - 2 < n:
            mixed[units[t - 2]] = mix(ps.pop(t - 2))
        if 0 <= t - 1 < n:
            ps[t - 1] = softmax_numerators(ss.pop(t - 1))
        yield
    lam = lam_ref[0]
    for sub in range(n_sub):
        o_ref[(first + sub) * rows:(first + sub + 1) * rows, :] = mixed[(sub, 0)] - lam * mixed[(sub, 1)]
    yield


def _attn_kernel(lam_ref, q_ref, *refs, n_sub):
    _emit_interleaved(_attn_stages(lam_ref, q_ref, refs[:-1], refs[-1], n_sub))


def _attention(lam, p_q, kv_sources):
    bsz, t, _ = p_q.shape
    tq = min(t, ATTN_ROW_TILE)
    n_src = len(kv_sources)
    in_specs = [
        pl.BlockSpec(memory_space=pltpu.SMEM),
        pl.BlockSpec((None, tq, HEAD_W), lambda bi, h, i: (bi, i, PB_Q * N_HEADS + h)),
    ]
    args = [lam, p_q]
    for src in kv_sources:
        tk = src.shape[1]
        in_specs.append(pl.BlockSpec((None, tk, HEAD_W), lambda bi, h, i: (bi, 0, PB_K * N_HEADS + h)))
        in_specs.append(pl.BlockSpec((None, tk, HEAD_W), lambda bi, h, i: (bi, 0, PB_V * N_HEADS + h)))
        args += [src, src]
    return pl.pallas_call(
        functools.partial(_attn_kernel, n_sub=max(tq // ATTN_SUB_ROWS, 1)),
        out_shape=jax.ShapeDtypeStruct((bsz, t, GROUP_W), F32),
        grid=(bsz, N_HEADS, t // tq),
        in_specs=in_specs,
        out_specs=pl.BlockSpec((None, tq, HEAD_W), lambda bi, h, i: (bi, i, h)),
        compiler_params=_params("parallel", "parallel", "arbitrary"),
        name="diff_attn",
    )(*args)


def _time_cumsum(x, tri):
    hi = x.astype(BF16)
    r1 = x - hi.astype(F32)
    mid = r1.astype(BF16)
    lo = (r1 - mid.astype(F32)).astype(BF16)
    parts = [jnp.dot(tri, part, preferred_element_type=F32) for part in (hi, mid, lo)]
    return parts[0] + (parts[1] + parts[2])


def _block_row(g, blk, idx):
    n, k = g.shape
    if blk >= 8:
        g4 = g.reshape(n // blk, blk // 8, 8, k)
        ref = g4[:, idx // 8:idx // 8 + 1, idx % 8:idx % 8 + 1, :]
        return jnp.broadcast_to(ref, g4.shape).reshape(n, k)
    g3 = g.reshape(n // 8, 8, k)
    sub = lax.broadcasted_iota(jnp.int32, (1, 8, 1), 1) // blk
    out = None
    for j in range(8 // blk):
        ref = jnp.broadcast_to(g3[:, j * blk + idx:j * blk + idx + 1, :], g3.shape)
        out = ref if out is None else jnp.where(sub == j, ref, out)
    return out.reshape(n, k)


def _hgrn_refs(refs, with_output):
    ins = [refs[0:5], refs[5:10]]
    outs = refs[10:]
    o_refs, s_refs = (outs[0:2], outs[2:4]) if with_output else (None, outs[0:2])
    return ins, o_refs, s_refs


def _hgrn_load_state(ins, s_refs, chunk_axis):
    @pl.when(pl.program_id(chunk_axis) == 0)
    def _():
        for d in range(2):
            s_refs[d][...] = ins[d][4][...]


def _hgrn_kernel(*refs, with_output):
    ins, o_refs, s_refs = _hgrn_refs(refs, with_output)
    _hgrn_load_state(ins, s_refs, 1)
    _emit_interleaved(_hgrn_stages(ins, o_refs, s_refs))


def _hgrn_stages(ins, o_refs, s_refs, rows=(slice(0, HG_CHUNK), slice(0, HG_CHUNK))):
    with_output = o_refs is not None
    n = HG_CHUNK
    n_levels = n.bit_length() - 1
    rt = lax.broadcasted_iota(jnp.int32, (n, n), 0)
    cs = lax.broadcasted_iota(jnp.int32, (n, n), 1)
    tri = [(cs <= rt).astype(BF16), (cs >= rt).astype(BF16)]
    last = [n - 1, 0]
    if with_output:
        differ = rt ^ cs
        top_bit = functools.reduce(jnp.add, [(differ >= (1 << b)).astype(jnp.int32) for b in range(1, n_levels)])
        level = [jnp.where(earlier, top_bit + 1, jnp.where(rt == cs, 0, -1)) for earlier in (cs < rt, cs > rt)]
    chains = [(d, h) for d in range(2) for h in range(N_HEADS)]
    lanes = [slice(h * HEAD_W, (h + 1) * HEAD_W) for h in range(N_HEADS)]
    kk, g = {}, {}
    for d, h in chains:
        lb = ins[d][3][h:h + 1, :]
        f = lb + (1.0 - lb) * jax.nn.sigmoid(ins[d][1][rows[d], lanes[h]])
        kk[d, h] = 1.0 - f
        g[d, h] = _time_cumsum(jnp.log2(f), tri[d])
    yield
    if with_output:
        qq = {(d, h): _silu(ins[d][0][rows[d], lanes[h]].astype(F32)) for d, h in chains}
        qb = {c: qq[c].astype(BF16) for c in chains}
        kb = {c: kk[c].astype(BF16) for c in chains}
        att = {(d, h): jnp.where(level[d] == 0,
                                 lax.dot_general(qb[d, h], kb[d, h], NT_DIMS, preferred_element_type=F32), 0.0)
               for d, h in chains}
        for lv in range(1, n_levels + 1):
            half = 1 << (lv - 1)
            for d, h in chains:
                boundary = _block_row(g[d, h], 2 * half, half if d == 1 else half - 1)
                e = jnp.exp2(-jnp.abs(g[d, h] - boundary)).astype(BF16)
                p = lax.dot_general(qb[d, h] * e, kb[d, h] * e, NT_DIMS, preferred_element_type=F32)
                att[d, h] = jnp.where(level[d] == lv, p, att[d, h])
            yield
        for d, h in chains:
            o = jnp.dot(att[d, h].astype(BF16), ins[d][2][rows[d], lanes[h]], preferred_element_type=F32)
            o = o + lax.dot_general((qq[d, h] * jnp.exp2(g[d, h])).astype(BF16), s_refs[d][h].astype(BF16),
                                    NT_DIMS, preferred_element_type=F32)
            o_refs[d][rows[d], lanes[h]] = o
        yield
    for d, h in chains:
        g_last = g[d, h][last[d]:last[d] + 1, :]
        kx = (kk[d, h] * jnp.exp2(g_last - g[d, h])).astype(BF16)
        s_refs[d][h] = jnp.exp2(g_last) * s_refs[d][h] + lax.dot_general(
            ins[d][2][rows[d], lanes[h]], kx, TN_DIMS, preferred_element_type=F32)


def _hgrn_operands(p_b, p_f, lb_f, lb_b, s0_f, s0_b, with_output, chunks_per_step=1):
    bsz, t, _ = p_b.shape
    c = HG_CHUNK * chunks_per_step
    nc = t // c
    cmaps = (lambda ci: ci, lambda ci: nc - 1 - ci)
    col = lambda d, blk: (lambda bi, ci: (bi, cmaps[d](ci), blk))
    state_spec = pl.BlockSpec((None, N_HEADS, HEAD_W, HEAD_W), lambda bi, ci: (bi, 0, 0, 0))
    state_shape = jax.ShapeDtypeStruct((bsz, N_HEADS, HEAD_W, HEAD_W), F32)
    in_specs, args = [], []
    for d, (lb, s0) in enumerate(((lb_f, s0_f), (lb_b, s0_b))):
        in_specs += [
            pl.BlockSpec((None, c, GROUP_W), col(d, PB_HQ)),
            pl.BlockSpec((None, c, GROUP_W), col(d, d)),
            pl.BlockSpec((None, c, GROUP_W), col(d, PB_HI)),
            pl.BlockSpec((N_HEADS, HEAD_W), lambda bi, ci: (0, 0)),
            state_spec,
        ]
        args += [p_b, p_f, p_b, lb, s0]
    out_shape, out_specs = [state_shape, state_shape], [state_spec, state_spec]
    if with_output:
        out_shape = [jax.ShapeDtypeStruct((bsz, t, GROUP_W), F32)] * 2 + out_shape
        out_specs = [pl.BlockSpec((None, c, GROUP_W), col(d, 0)) for d in range(2)] + out_specs
    return in_specs, args, out_shape, out_specs


def _hgrn_scan(p_b, p_f, lb_f, lb_b, s0_f, s0_b, with_output):
    in_specs, args, out_shape, out_specs = _hgrn_operands(p_b, p_f, lb_f, lb_b, s0_f, s0_b, with_output)
    res = pl.pallas_call(
        functools.partial(_hgrn_kernel, with_output=with_output),
        out_shape=out_shape,
        grid=(p_b.shape[0], p_b.shape[1] // HG_CHUNK),
        in_specs=in_specs,
        out_specs=out_specs,
        compiler_params=_params("parallel", "arbitrary"),
        name="hgrn",
    )(*args)
    return tuple(res) if with_output else (None, None, res[0], res[1])


def _mixer_kernel(*refs, n_attn_in, n_sub):
    attn_in, hgrn_in, outs = refs[:n_attn_in], refs[n_attn_in:n_attn_in + 10], refs[n_attn_in + 10:]
    ins, o_refs, s_refs = _hgrn_refs(hgrn_in + outs[1:], True)
    _hgrn_load_state(ins, s_refs, 1)
    for k in range(MIXER_CHUNKS):
        back = MIXER_CHUNKS - 1 - k
        rows = (slice(k * HG_CHUNK, (k + 1) * HG_CHUNK), slice(back * HG_CHUNK, (back + 1) * HG_CHUNK))
        _emit_interleaved(
            _attn_stages(attn_in[0], attn_in[1], attn_in[2:], outs[0], n_sub, tile=k, n_tiles=MIXER_CHUNKS),
            _hgrn_stages(ins, o_refs, s_refs, rows))


def _latent_mixer(lam, pb_l, pf_l, pb_c, lb_f, lb_b, s0_f, s0_b):
    bsz, t, _ = pb_l.shape
    nc = t // (HG_CHUNK * MIXER_CHUNKS)
    tq = t * N_HEADS // nc
    nq = t // tq
    head_col = lambda grp: (lambda bi, ci: (bi, 0, grp * N_HEADS + ci // nq))
    in_specs = [
        pl.BlockSpec(memory_space=pltpu.SMEM),
        pl.BlockSpec((None, tq, HEAD_W), lambda bi, ci: (bi, ci % nq, PB_Q * N_HEADS + ci // nq)),
    ]
    args = [lam, pb_l]
    for src in (pb_l, pb_c):
        in_specs += [pl.BlockSpec((None, src.shape[1], HEAD_W), head_col(PB_K)),
                     pl.BlockSpec((None, src.shape[1], HEAD_W), head_col(PB_V))]
        args += [src, src]
    n_attn_in = len(args)
    h_specs, h_args, h_shape, h_out_specs = _hgrn_operands(pb_l, pf_l, lb_f, lb_b, s0_f, s0_b, True, MIXER_CHUNKS)
    res = pl.pallas_call(
        functools.partial(_mixer_kernel, n_attn_in=n_attn_in,
                          n_sub=max(tq // (MIXER_CHUNKS * ATTN_SUB_ROWS), 1)),
        out_shape=[jax.ShapeDtypeStruct((bsz, t, GROUP_W), F32)] + h_shape,
        grid=(bsz, nc),
        in_specs=in_specs + h_specs,
        out_specs=[pl.BlockSpec((None, tq, HEAD_W), lambda bi, ci: (bi, ci % nq, ci // nq))] + h_out_specs,
        compiler_params=_params("parallel", "arbitrary"),
        name="latent_mixer",
    )(*args, *h_args)
    return res[0], res[1], res[2]


def _merge_kernel(x_ref, mod_ref, a_ref, of_ref, ob_ref, gt_ref, dn_ref, hn_ref, w_ref, g_ref, b_ref, o_ref,
                  *, da_scale):
    a = a_ref[...]
    o = of_ref[...] + ob_ref[...]
    parts_a, parts_o = [], []
    for h in range(N_HEADS):
        sl = slice(h * HEAD_W, (h + 1) * HEAD_W)
        parts_a.append(_rms_norm(a[:, sl], dn_ref[:, sl]) * da_scale)
        parts_o.append(_rms_norm(o[:, sl], hn_ref[:, sl]))
    ya = jnp.concatenate(parts_a, axis=1)
    yo = jnp.concatenate(parts_o, axis=1) * _silu(gt_ref[...].astype(F32))
    y = jnp.dot(jnp.concatenate([ya, yo], axis=1).astype(BF16), w_ref[...], preferred_element_type=F32)
    z = ALPHA * x_ref[...] + mod_ref[5:6, :] * y
    o_ref[...] = _layer_norm(z, g_ref[1:2, :], b_ref[1:2, :])


def _merge(x, mods, layer, mod_row, attn, o_f, o_b, p_b, da_norm, hg_norm, w_out, j, ln_g, ln_b, da_scale):
    bsz, t, d = x.shape
    tm = _row_tile(t)
    grp = pl.BlockSpec((None, tm, GROUP_W), lambda bi, i: (bi, i, 0))
    vec = lambda n: pl.BlockSpec((None, 1, n), lambda bi, i: (j, 0, 0))
    ln = pl.BlockSpec((None, 3, d), lambda bi, i: (layer, 0, 0))
    return pl.pallas_call(
        functools.partial(_merge_kernel, da_scale=da_scale),
        out_shape=jax.ShapeDtypeStruct(x.shape, F32),
        grid=(bsz, t // tm),
        in_specs=[
            pl.BlockSpec((None, tm, d), lambda bi, i: (bi, i, 0)),
            _mod_spec(layer, mod_row, 2),
            grp, grp, grp,
            pl.BlockSpec((None, tm, GROUP_W), lambda bi, i: (bi, i, PB_HG)),
            vec(GROUP_W), vec(GROUP_W),
            pl.BlockSpec((None, d, d), lambda bi, i: (j, 0, 0)),
            ln, ln,
        ],
        out_specs=pl.BlockSpec((None, tm, d), lambda bi, i: (bi, i, 0)),
        compiler_params=_params("parallel", "parallel"),
        name="even_merge",
    )(x, mods, attn, o_f, o_b, p_b, da_norm, hg_norm, w_out, ln_g, ln_b)


def _glu_kernel(x_ref, mod_ref, w_ref, b_ref, o_ref):
    h = (x_ref[...] * (1.0 + mod_ref[4:5, :]) + mod_ref[3:4, :]).astype(BF16)
    y = jnp.dot(h, w_ref[...], preferred_element_type=F32) + b_ref[...]
    d = o_ref.shape[-1]
    o_ref[...] = y[:, :d] * jax.nn.sigmoid(y[:, d:])


def _glu(x, mods, layer, mod_row, w_pw1, b_pw1, j):
    bsz, t, d = x.shape
    tm = _row_tile(t)
    return pl.pallas_call(
        _glu_kernel,
        out_shape=jax.ShapeDtypeStruct(x.shape, F32),
        grid=(bsz, t // tm),
        in_specs=[
            pl.BlockSpec((None, tm, d), lambda bi, i: (bi, i, 0)),
            _mod_spec(layer, mod_row, 2),
            pl.BlockSpec((None, d, 2 * d), lambda bi, i: (j, 0, 0)),
            pl.BlockSpec((None, 1, 2 * d), lambda bi, i: (j, 0, 0)),
        ],
        out_specs=pl.BlockSpec((None, tm, d), lambda bi, i: (bi, i, 0)),
        compiler_params=_params("parallel", "parallel"),
        name="conv_glu",
    )(x, mods, w_pw1, b_pw1)


def _conv_kernel(x_ref, mod_ref, u_ref, up_ref, un_ref, wdw_ref, bdw_ref, cg_ref, cb_ref, w_ref, b2_ref,
                 g_ref, b_ref, o_ref, ext_ref, sh_ref, acc_ref):
    i = pl.program_id(1)
    tm = u_ref.shape[0]
    ext_ref[0:CONV_HALO, :] = jnp.where(i > 0, up_ref[...], 0.0)
    ext_ref[CONV_HALO:CONV_HALO + tm, :] = u_ref[...]
    ext_ref[CONV_HALO + tm:, :] = jnp.where(i < pl.num_programs(1) - 1, un_ref[...], 0.0)
    off = CONV_HALO - CONV_WIDTH // 2
    span = sh_ref.shape[1]
    for cb in range(u_ref.shape[1] // HEAD_W):
        cols = slice(cb * HEAD_W, (cb + 1) * HEAD_W)
        for p in range(1, 8):
            sh_ref[p - 1] = ext_ref[p:p + span, cols]

        def row_block(rb, carry):
            r0 = pl.multiple_of(rb * CONV_ROWS, CONV_ROWS)
            partial = [None] * CONV_PARTIALS
            for w in range(CONV_WIDTH):
                p, a = (off + w) % 8, (off + w) // 8
                rows = pl.ds(r0 + 8 * a, CONV_ROWS)
                win = ext_ref[rows, cols] if p == 0 else sh_ref[p - 1, rows, :]
                term = win * wdw_ref[w:w + 1, cols]
                k = w % CONV_PARTIALS
                partial[k] = term if partial[k] is None else partial[k] + term
            while len(partial) > 1:
                partial = [a + b for a, b in zip(partial[0::2], partial[1::2])]
            acc_ref[pl.ds(r0, CONV_ROWS), cols] = partial[0]
            return carry

        lax.fori_loop(0, tm // CONV_ROWS, row_block, 0)
    u = _silu(_layer_norm(acc_ref[...] + bdw_ref[...], cg_ref[...], cb_ref[...]))
    y = jnp.dot(u.astype(BF16), w_ref[...], preferred_element_type=F32) + b2_ref[...]
    z = ALPHA * x_ref[...] + mod_ref[5:6, :] * y
    o_ref[...] = _layer_norm(z, g_ref[1:2, :], b_ref[1:2, :])


def _conv(x, mods, layer, mod_row, u, w_dw, b_dw, cg, cb, w_pw2, b_pw2, j, ln_g, ln_b):
    bsz, t, d = x.shape
    tm = _row_tile(t)
    r = tm // CONV_HALO
    n_halo = t // CONV_HALO
    row = pl.BlockSpec((None, tm, d), lambda bi, i: (bi, i, 0))
    vec = pl.BlockSpec((None, 1, d), lambda bi, i: (j, 0, 0))
    ln = pl.BlockSpec((None, 3, d), lambda bi, i: (layer, 0, 0))
    return pl.pallas_call(
        _conv_kernel,
        out_shape=jax.ShapeDtypeStruct(x.shape, F32),
        grid=(bsz, t // tm),
        in_specs=[
            row,
            _mod_spec(layer, mod_row, 2),
            row,
            pl.BlockSpec((None, CONV_HALO, d), lambda bi, i: (bi, jnp.maximum(i * r - 1, 0), 0)),
            pl.BlockSpec((None, CONV_HALO, d), lambda bi, i: (bi, jnp.minimum((i + 1) * r, n_halo - 1), 0)),
            pl.BlockSpec((None, CONV_WIDTH, d), lambda bi, i: (j, 0, 0)),
            vec, vec, vec,
            pl.BlockSpec((None, d, d), lambda bi, i: (j, 0, 0)),
            vec, ln, ln,
        ],
        out_specs=row,
        scratch_shapes=[pltpu.VMEM((tm + 2 * CONV_HALO, d), F32),
                        pltpu.VMEM((7, tm + 8 * ((CONV_WIDTH + 7) // 8 - 1), HEAD_W), F32),
                        pltpu.VMEM((tm, d), F32)],
        compiler_params=_params("parallel", "parallel"),
        name="conv_mix",
    )(x, mods, u, u, u, w_dw, b_dw, cg, cb, w_pw2, b_pw2, ln_g, ln_b)


def kernel(x, c, ctx, c_ctx, w_mod, b_mod, ln_g, ln_b, ffn_w13, ffn_w2, ev_w_in, ev_w_out, da_lambda, da_norm,
           hg_lb, hg_norm, cv_w_pw1, cv_b_pw1, cv_w_dw, cv_b_dw, cv_ln_g, cv_ln_b, cv_w_pw2, cv_b_pw2):
    bsz, seq, d = x.shape
    per_layer_row = lambda a: a.reshape(a.shape[0], 1, a.shape[1])

    cc = jnp.zeros((16, d), F32).at[:bsz].set(c).at[bsz].set(c_ctx)
    mods = _mods(cc, w_mod, b_mod).reshape(DEPTH, 16, N_MOD, d)
    lat_row = lambda bi: bi
    ctx_row = lambda bi: bsz
    rope_tabs = _rope_tables(seq)
    lb_w = jax.nn.softmax(hg_lb.astype(F32), axis=0)
    lower_bounds = jnp.cumsum(lb_w, axis=0) - lb_w[:1]

    w13, w2 = ffn_w13.astype(BF16), ffn_w2.astype(BF16)
    w_in, w_out = ev_w_in.astype(BF16), ev_w_out.astype(BF16)
    w_pw1, w_pw2 = cv_w_pw1.astype(BF16), cv_w_pw2.astype(BF16)
    da_norm3, hg_norm3 = per_layer_row(da_norm), per_layer_row(hg_norm)
    conv_vecs = [per_layer_row(a) for a in (cv_b_dw, cv_ln_g, cv_ln_b)]
    b_pw1, b_pw2 = per_layer_row(cv_b_pw1), per_layer_row(cv_b_pw2)

    xl, xc = x, ctx
    for layer in range(DEPTH):
        j = layer // 2
        even = layer % 2 == 0
        ctx_pre = layer <= LAST_CTX_READ
        ctx_full = layer < LAST_CTX_READ
        ffn = lambda xx, mrow, fi, sub: _ffn(xx, mods, layer, mrow, w13, w2, fi, ln_g, ln_b, sub)

        xl = ffn(xl, lat_row, 0, 0)
        if ctx_pre:
            xc = ffn(xc, ctx_row, 0, 0)

        if even:
            lam_init = 0.8 - 0.6 * math.exp(-0.3 * layer)
            lv = da_lambda[j].astype(F32)
            lam = (jnp.exp(jnp.sum(lv[0] * lv[1])) - jnp.exp(jnp.sum(lv[2] * lv[3])) + lam_init).reshape(1)
            lb_f = lower_bounds[j, :GROUP_W].reshape(N_HEADS, HEAD_W)
            lb_b = lower_bounds[j, GROUP_W:].reshape(N_HEADS, HEAD_W)
            pb_l, pf_l = _inproj(xl, mods, layer, lat_row, w_in, j, rope_tabs)
            pb_c, pf_c = _inproj(xc, mods, layer, ctx_row, w_in, j, None)
            zeros = jnp.zeros((bsz, N_HEADS, HEAD_W, HEAD_W), F32)
            ocf, ocb, s_f, s_b = _hgrn_scan(pb_c, pf_c, lb_f, lb_b, zeros, zeros, ctx_full)
            a_l, olf, olb = _latent_mixer(lam, pb_l, pf_l, pb_c, lb_f, lb_b, s_f, s_b)
            merge = lambda xx, mrow, a, o_f, o_b, pb: _merge(
                xx, mods, layer, mrow, a, o_f, o_b, pb, da_norm3, hg_norm3, w_out, j, ln_g, ln_b, 1.0 - lam_init)
            if ctx_full:
                a_c = _attention(lam, pb_c, [pb_c])
                xc = merge(xc, ctx_row, a_c, ocf, ocb, pb_c)
            xl = merge(xl, lat_row, a_l, olf, olb, pb_l)
        else:
            conv = lambda xx, mrow, u: _conv(xx, mods, layer, mrow, u, cv_w_dw, *conv_vecs, w_pw2, b_pw2, j,
                                             ln_g, ln_b)
            if ctx_full:
                xc = conv(xc, ctx_row, _glu(xc, mods, layer, ctx_row, w_pw1, b_pw1, j))
            xl = conv(xl, lat_row, _glu(xl, mods, layer, lat_row, w_pw1, b_pw1, j))

        xl = ffn(xl, lat_row, 1, 2)
        if ctx_full:
            xc = ffn(xc, ctx_row, 1, 2)
    return xl
```

```python
import functools
import math

import jax
import jax.numpy as jnp
from jax import lax
from jax.experimental import pallas as pl
from jax.experimental.pallas import tpu as pltpu

F32 = jnp.float32
BF16 = jnp.bfloat16

D_MODEL = 1024
DEPTH = 4
GRID_W = 64
N_HEADS = 4
HEAD_W = 128
QK_DIM = 64
GROUP_W = N_HEADS * HEAD_W
D_FF = 2816
N_MOD = 9
CONV_WIDTH = 31
CONV_HALO = 16
ROPE_BASE = 10000.0
LAST_CTX_READ = 2 * ((DEPTH - 1) // 2)
ALPHA = (2 * DEPTH) ** 0.25
EPS = 1e-5
HG_CHUNK = 128
MIXER_CHUNKS = 4
MXU_WIDTH = 256
FFN_SPLIT = (D_FF // MXU_WIDTH + 1) // 2 * MXU_WIDTH
FFN_CHUNKS = ((0, FFN_SPLIT), (FFN_SPLIT, D_FF))
ROW_TILE = 512
ATTN_ROW_TILE = 1024
ATTN_SUB_ROWS = 256
CONV_ROWS = 64
CONV_PARTIALS = 4
VMEM_LIMIT = 56 * 1024 * 1024

PB_WIDTH = 6 * GROUP_W
PF_WIDTH = 2 * GROUP_W
PB_Q, PB_K, PB_V, PB_HQ, PB_HI, PB_HG = range(6)

NT_DIMS = (((1,), (1,)), ((), ()))
TN_DIMS = (((0,), (0,)), ((), ()))


def _params(*sem):
    return pltpu.CompilerParams(dimension_semantics=sem, vmem_limit_bytes=VMEM_LIMIT)


def _layer_norm(z, g, b):
    mu = jnp.mean(z, axis=-1, keepdims=True)
    zc = z - mu
    var = jnp.mean(zc * zc, axis=-1, keepdims=True)
    return zc * lax.rsqrt(var + EPS) * g + b


def _rms_norm(z, g):
    return z * lax.rsqrt(jnp.mean(z * z, axis=-1, keepdims=True) + EPS) * g


def _silu(a):
    return a * jax.nn.sigmoid(a)


def _row_tile(t):
    return min(t, ROW_TILE)


def _mod_spec(layer, mod_row, grid_rank):
    if grid_rank == 2:
        return pl.BlockSpec((None, None, N_MOD, D_MODEL), lambda bi, i: (layer, mod_row(bi), 0, 0))
    return pl.BlockSpec((None, None, N_MOD, D_MODEL), lambda bi, i, j: (layer, mod_row(bi), 0, 0))


def _mods_kernel(c_ref, w_ref, b_ref, o_ref):
    a = _silu(c_ref[...]).astype(BF16)
    o_ref[...] = jnp.dot(a, w_ref[...].astype(BF16), preferred_element_type=F32) + b_ref[...]


def _mods(cc, w_mod, b_mod):
    n = cc.shape[0]
    tn = 1024
    return pl.pallas_call(
        _mods_kernel,
        out_shape=jax.ShapeDtypeStruct((DEPTH, n, N_MOD * D_MODEL), F32),
        grid=(DEPTH, N_MOD * D_MODEL // tn),
        in_specs=[
            pl.BlockSpec((n, D_MODEL), lambda l, j: (0, 0)),
            pl.BlockSpec((None, D_MODEL, tn), lambda l, j: (l, 0, j)),
            pl.BlockSpec((None, 1, tn), lambda l, j: (l, 0, j)),
        ],
        out_specs=pl.BlockSpec((None, n, tn), lambda l, j: (l, 0, j)),
        compiler_params=_params("parallel", "parallel"),
        name="mods",
    )(cc, w_mod, b_mod.reshape(DEPTH, 1, N_MOD * D_MODEL))


def _ffn_kernel(xn_ref, xp_ref, modn_ref, modp_ref, w13_ref, w2_ref, g_ref, b_ref, o_ref, h_scr, acc_scr,
                *, sub, n_tiles):
    s = pl.program_id(0)

    def prologue(dst):
        shift = modn_ref[3 * sub:3 * sub + 1, :]
        scale = modn_ref[3 * sub + 1:3 * sub + 2, :]
        h_scr[dst] = (xn_ref[...] * (1.0 + scale) + shift).astype(BF16)

    def epilogue(src):
        gate = modp_ref[3 * sub + 2:3 * sub + 3, :]
        z = ALPHA * xp_ref[...] + gate * (0.5 * acc_scr[src])
        o_ref[...] = _layer_norm(z, g_ref[sub:sub + 1, :], b_ref[sub:sub + 1, :])

    def matmuls(slot):
        h = h_scr[slot]
        ab = [(jnp.dot(h, w13_ref[:, lo:hi], preferred_element_type=F32),
               jnp.dot(h, w13_ref[:, D_FF + lo:D_FF + hi], preferred_element_type=F32)) for lo, hi in FFN_CHUNKS]
        ys = [jnp.dot((_silu(a) * b).astype(BF16), w2_ref[lo:hi, :], preferred_element_type=F32)
              for (a, b), (lo, hi) in zip(ab, FFN_CHUNKS)]
        acc_scr[slot] = functools.reduce(jnp.add, ys)

    @pl.when(s == 0)
    def _():
        prologue(0)
        acc_scr[1] = jnp.zeros(acc_scr.shape[1:], F32)

    for parity in range(2):
        @pl.when((s >= 1) & (s <= n_tiles) & (s % 2 == parity))
        def _():
            epilogue(parity)
            prologue(parity)
            matmuls(1 - parity)

    @pl.when(s == n_tiles + 1)
    def _():
        epilogue((n_tiles + 1) % 2)


def _ffn(x, mods, layer, mod_row, w13, w2, fi, ln_g, ln_b, sub):
    bsz, t, d = x.shape
    tm = _row_tile(t)
    per_b = t // tm
    n = bsz * per_b
    xf = x.reshape(bsz * t, d)
    nxt = lambda s: jnp.minimum(s, n - 1)
    prv = lambda s: jnp.clip(s - 2, 0, n - 1)
    resident = dict(pipeline_mode=pl.Buffered(1))
    out = pl.pallas_call(
        functools.partial(_ffn_kernel, sub=sub, n_tiles=n),
        out_shape=jax.ShapeDtypeStruct(xf.shape, F32),
        grid=(n + 2,),
        in_specs=[
            pl.BlockSpec((tm, d), lambda s: (nxt(s), 0)),
            pl.BlockSpec((tm, d), lambda s: (prv(s), 0)),
            pl.BlockSpec((None, None, N_MOD, d), lambda s: (layer, mod_row(nxt(s) // per_b), 0, 0)),
            pl.BlockSpec((None, None, N_MOD, d), lambda s: (layer, mod_row(prv(s) // per_b), 0, 0)),
            pl.BlockSpec((None, None, d, 2 * D_FF), lambda s: (layer, fi, 0, 0), **resident),
            pl.BlockSpec((None, None, D_FF, d), lambda s: (layer, fi, 0, 0), **resident),
            pl.BlockSpec((None, 3, d), lambda s: (layer, 0, 0)),
            pl.BlockSpec((None, 3, d), lambda s: (layer, 0, 0)),
        ],
        out_specs=pl.BlockSpec((tm, d), lambda s: (prv(s), 0)),
        scratch_shapes=[pltpu.VMEM((2, tm, d), BF16), pltpu.VMEM((2, tm, d), F32)],
        compiler_params=_params("arbitrary"),
        name="ffn",
    )(xf, xf, mods, mods, w13, w2, ln_g, ln_b)
    return out.reshape(x.shape)


def _inproj_kernel(*refs, rope):
    if rope:
        x_ref, mod_ref, w_ref, cos_ref, sa_ref, sb_ref, ob_ref, of_ref = refs
    else:
        x_ref, mod_ref, w_ref, ob_ref, of_ref = refs
    h = (x_ref[...] * (1.0 + mod_ref[4:5, :]) + mod_ref[3:4, :]).astype(BF16)
    w2 = 2 * GROUP_W
    proj = lambda grp: jnp.dot(h, w_ref[:, grp * w2:(grp + 1) * w2], preferred_element_type=F32)
    y = proj(0)
    if rope:
        wide = lambda r: jnp.concatenate([r[...]] * (w2 // HEAD_W), axis=1)
        y = (y * wide(cos_ref) + pltpu.roll(y, w2 - QK_DIM // 4, 1) * wide(sa_ref)
             + pltpu.roll(y, QK_DIM // 4, 1) * wide(sb_ref))
    ob_ref[:, 0:w2] = y.astype(BF16)
    ob_ref[:, w2:2 * w2] = proj(1).astype(BF16)
    of_ref[...] = proj(2)
    ob_ref[:, 2 * w2:3 * w2] = proj(3).astype(BF16)


def _inproj(x, mods, layer, mod_row, w_in, j, rope_tabs):
    bsz, t, d = x.shape
    tm = _row_tile(t)
    rope = rope_tabs is not None
    in_specs = [
        pl.BlockSpec((None, tm, d), lambda bi, i: (bi, i, 0)),
        _mod_spec(layer, mod_row, 2),
        pl.BlockSpec((None, d, 8 * GROUP_W), lambda bi, i: (j, 0, 0)),
    ]
    args = [x, mods, w_in]
    if rope:
        in_specs += [pl.BlockSpec((tm, HEAD_W), lambda bi, i: (i, 0))] * 3
        args += list(rope_tabs)
    return pl.pallas_call(
        functools.partial(_inproj_kernel, rope=rope),
        out_shape=[jax.ShapeDtypeStruct((bsz, t, PB_WIDTH), BF16), jax.ShapeDtypeStruct((bsz, t, PF_WIDTH), F32)],
        grid=(bsz, t // tm),
        in_specs=in_specs,
        out_specs=[pl.BlockSpec((None, tm, PB_WIDTH), lambda bi, i: (bi, i, 0)),
                   pl.BlockSpec((None, tm, PF_WIDTH), lambda bi, i: (bi, i, 0))],
        compiler_params=_params("parallel", "parallel"),
        name="inproj",
    )(*args)


def _rope_tables(n_tokens):
    pos = jnp.arange(n_tokens)
    half = QK_DIM // 2
    inv_freq = ROPE_BASE ** (-jnp.arange(0, half, 2, dtype=F32) / half)

    def table(p):
        ang = p.astype(F32)[:, None] * inv_freq[None, :]
        ang = jnp.concatenate([ang, ang], axis=-1)
        return jnp.cos(ang), jnp.sin(ang)

    (cos_r, sin_r), (cos_c, sin_c) = table(pos // GRID_W), table(pos % GRID_W)
    cos = jnp.concatenate([cos_r, cos_c, cos_r, cos_c], axis=-1)
    sin = jnp.concatenate([sin_r, sin_c, sin_r, sin_c], axis=-1)
    first = (jnp.arange(HEAD_W) % half) < half // 2
    return cos, jnp.where(first, -sin, 0.0), jnp.where(first, 0.0, sin)


def _emit_interleaved(*stage_lists):
    live = list(stage_lists)
    while live:
        for stages in list(live):
            if next(stages, StopIteration) is StopIteration:
                live.remove(stages)


def _with_ones(v):
    return jnp.concatenate([v, jnp.ones(v.shape, BF16)], axis=1)


def _attn_stages(lam_ref, q_ref, k_refs, vext_refs, o_ref, n_sub, tile=0, n_tiles=1):
    rows = q_ref.shape[0] // (n_tiles * n_sub)
    first = tile * n_sub
    lane = lax.broadcasted_iota(jnp.int32, (1, HEAD_W), 1)
    units = [(sub, comp) for sub in range(n_sub) for comp in range(2)]

    def scores(unit):
        sub, comp = unit
        q = q_ref[(first + sub) * rows:(first + sub + 1) * rows, :] * (QK_DIM ** -0.5)
        sel = (lane < QK_DIM) if comp == 0 else (lane >= QK_DIM)
        qc = jnp.where(sel, q, jnp.zeros_like(q))
        return [lax.dot_general(qc, k[...], NT_DIMS, preferred_element_type=F32) for k in k_refs]

    def softmax_numerators(ss):
        m = functools.reduce(jnp.maximum, [jnp.max(s, axis=-1, keepdims=True) for s in ss])
        return [jnp.exp((s - m).astype(BF16)) for s in ss]

    def mix(ps):
        acc = functools.reduce(jnp.add, [jnp.dot(p, v[...], preferred_element_type=F32)
                                         for p, v in zip(ps, vext_refs)])
        return acc[:, :HEAD_W] / acc[:, HEAD_W:HEAD_W + 1]

    n = len(units)
    ss, ps, mixed = {}, {}, {}
    for t in range(n + 2):
        if t < n:
            ss[t] = scores(units[t])
        if 0 <= t - 2 < n:
            mixed[units[t - 2]] = mix(ps.pop(t - 2))
        if 0 <= t - 1 < n:
            ps[t - 1] = softmax_numerators(ss.pop(t - 1))
        yield
    lam = lam_ref[0]
    for sub in range(n_sub):
        o_ref[(first + sub) * rows:(first + sub + 1) * rows, :] = mixed[(sub, 0)] - lam * mixed[(sub, 1)]
    yield


def _fill_vext(v_refs, vext_refs):
    for v_ref, vext_ref in zip(v_refs, vext_refs):
        vext_ref[...] = _with_ones(v_ref[...])


def _attn_kernel(lam_ref, q_ref, *refs, n_src, n_sub):
    kv_refs, o_ref, vext_refs = refs[:2 * n_src], refs[2 * n_src], refs[2 * n_src + 1:]
    _fill_vext(kv_refs[1::2], vext_refs)
    _emit_interleaved(_attn_stages(lam_ref, q_ref, kv_refs[0::2], vext_refs, o_ref, n_sub))


def _attention(lam, p_q, kv_sources):
    bsz, t, _ = p_q.shape
    tq = min(t, ATTN_ROW_TILE)
    n_src = len(kv_sources)
    in_specs = [
        pl.BlockSpec(memory_space=pltpu.SMEM),
        pl.BlockSpec((None, tq, HEAD_W), lambda bi, h, i: (bi, i, PB_Q * N_HEADS + h)),
    ]
    args = [lam, p_q]
    for src in kv_sources:
        tk = src.shape[1]
        in_specs.append(pl.BlockSpec((None, tk, HEAD_W), lambda bi, h, i: (bi, 0, PB_K * N_HEADS + h)))
        in_specs.append(pl.BlockSpec((None, tk, HEAD_W), lambda bi, h, i: (bi, 0, PB_V * N_HEADS + h)))
        args += [src, src]
    return pl.pallas_call(
        functools.partial(_attn_kernel, n_src=n_src, n_sub=max(tq // ATTN_SUB_ROWS, 1)),
        out_shape=jax.ShapeDtypeStruct((bsz, t, GROUP_W), F32),
        grid=(bsz, N_HEADS, t // tq),
        in_specs=in_specs,
        out_specs=pl.BlockSpec((None, tq, HEAD_W), lambda bi, h, i: (bi, i, h)),
        scratch_shapes=[pltpu.VMEM((src.shape[1], 2 * HEAD_W), BF16) for src in kv_sources],
        compiler_params=_params("parallel", "parallel", "arbitrary"),
        name="diff_attn",
    )(*args)


def _time_cumsum(x, tri):
    hi = x.astype(BF16)
    r1 = x - hi.astype(F32)
    mid = r1.astype(BF16)
    lo = (r1 - mid.astype(F32)).astype(BF16)
    parts = [jnp.dot(tri, part, preferred_element_type=F32) for part in (hi, mid, lo)]
    return parts[0] + (parts[1] + parts[2])


def _block_row(g, blk, idx):
    n, k = g.shape
    if blk >= 8:
        g4 = g.reshape(n // blk, blk // 8, 8, k)
        ref = g4[:, idx // 8:idx // 8 + 1, idx % 8:idx % 8 + 1, :]
        return jnp.broadcast_to(ref, g4.shape).reshape(n, k)
    g3 = g.reshape(n // 8, 8, k)
    sub = lax.broadcasted_iota(jnp.int32, (1, 8, 1), 1) // blk
    out = None
    for j in range(8 // blk):
        ref = jnp.broadcast_to(g3[:, j * blk + idx:j * blk + idx + 1, :], g3.shape)
        out = ref if out is None else jnp.where(sub == j, ref, out)
    return out.reshape(n, k)


def _hgrn_refs(refs, with_output):
    ins = [refs[0:5], refs[5:10]]
    outs = refs[10:]
    o_refs, s_refs = (outs[0:2], outs[2:4]) if with_output else (None, outs[0:2])
    return ins, o_refs, s_refs


def _hgrn_load_state(ins, s_refs, chunk_axis):
    @pl.when(pl.program_id(chunk_axis) == 0)
    def _():
        for d in range(2):
            s_refs[d][...] = ins[d][4][...]


def _hgrn_kernel(*refs, with_output):
    ins, o_refs, s_refs = _hgrn_refs(refs, with_output)
    _hgrn_load_state(ins, s_refs, 1)
    _emit_interleaved(_hgrn_stages(ins, o_refs, s_refs))


def _hgrn_stages(ins, o_refs, s_refs, rows=(slice(0, HG_CHUNK), slice(0, HG_CHUNK))):
    with_output = o_refs is not None
    n = HG_CHUNK
    n_levels = n.bit_length() - 1
    rt = lax.broadcasted_iota(jnp.int32, (n, n), 0)
    cs = lax.broadcasted_iota(jnp.int32, (n, n), 1)
    tri = [(cs <= rt).astype(BF16), (cs >= rt).astype(BF16)]
    last = [n - 1, 0]
    if with_output:
        differ = rt ^ cs
        top_bit = functools.reduce(jnp.add, [(differ >= (1 << b)).astype(jnp.int32) for b in range(1, n_levels)])
        level = [jnp.where(earlier, top_bit + 1, jnp.where(rt == cs, 0, -1)) for earlier in (cs < rt, cs > rt)]
    chains = [(d, h) for d in range(2) for h in range(N_HEADS)]
    lanes = [slice(h * HEAD_W, (h + 1) * HEAD_W) for h in range(N_HEADS)]
    kk, g = {}, {}
    for d, h in chains:
        lb = ins[d][3][h:h + 1, :]
        f = lb + (1.0 - lb) * jax.nn.sigmoid(ins[d][1][rows[d], lanes[h]])
        kk[d, h] = 1.0 - f
        g[d, h] = _time_cumsum(jnp.log2(f), tri[d])
    yield
    if with_output:
        qq = {(d, h): _silu(ins[d][0][rows[d], lanes[h]].astype(F32)) for d, h in chains}
        qb = {c: qq[c].astype(BF16) for c in chains}
        kb = {c: kk[c].astype(BF16) for c in chains}
        att = {(d, h): jnp.where(level[d] == 0,
                                 lax.dot_general(qb[d, h], kb[d, h], NT_DIMS, preferred_element_type=F32), 0.0)
               for d, h in chains}
        for lv in range(1, n_levels + 1):
            half = 1 << (lv - 1)
            for d, h in chains:
                boundary = _block_row(g[d, h], 2 * half, half if d == 1 else half - 1)
                e = jnp.exp2(-jnp.abs(g[d, h] - boundary)).astype(BF16)
                p = lax.dot_general(qb[d, h] * e, kb[d, h] * e, NT_DIMS, preferred_element_type=F32)
                att[d, h] = jnp.where(level[d] == lv, p, att[d, h])
            yield
        for d, h in chains:
            o = jnp.dot(att[d, h].astype(BF16), ins[d][2][rows[d], lanes[h]], preferred_element_type=F32)
            o = o + lax.dot_general((qq[d, h] * jnp.exp2(g[d, h])).astype(BF16), s_refs[d][h].astype(BF16),
                                    NT_DIMS, preferred_element_type=F32)
            o_refs[d][rows[d], lanes[h]] = o
        yield
    for d, h in chains:
        g_last = g[d, h][last[d]:last[d] + 1, :]
        kx = (kk[d, h] * jnp.exp2(g_last - g[d, h])).astype(BF16)
        s_refs[d][h] = jnp.exp2(g_last) * s_refs[d][h] + lax.dot_general(
            ins[d][2][rows[d], lanes[h]], kx, TN_DIMS, preferred_element_type=F32)


def _hgrn_operands(p_b, p_f, lb_f, lb_b, s0_f, s0_b, with_output, chunks_per_step=1):
    bsz, t, _ = p_b.shape
    c = HG_CHUNK * chunks_per_step
    nc = t // c
    cmaps = (lambda ci: ci, lambda ci: nc - 1 - ci)
    col = lambda d, blk: (lambda bi, ci: (bi, cmaps[d](ci), blk))
    state_spec = pl.BlockSpec((None, N_HEADS, HEAD_W, HEAD_W), lambda bi, ci: (bi, 0, 0, 0))
    state_shape = jax.ShapeDtypeStruct((bsz, N_HEADS, HEAD_W, HEAD_W), F32)
    in_specs, args = [], []
    for d, (lb, s0) in enumerate(((lb_f, s0_f), (lb_b, s0_b))):
        in_specs += [
            pl.BlockSpec((None, c, GROUP_W), col(d, PB_HQ)),
            pl.BlockSpec((None, c, GROUP_W), col(d, d)),
            pl.BlockSpec((None, c, GROUP_W), col(d, PB_HI)),
            pl.BlockSpec((N_HEADS, HEAD_W), lambda bi, ci: (0, 0)),
            state_spec,
        ]
        args += [p_b, p_f, p_b, lb, s0]
    out_shape, out_specs = [state_shape, state_shape], [state_spec, state_spec]
    if with_output:
        out_shape = [jax.ShapeDtypeStruct((bsz, t, GROUP_W), F32)] * 2 + out_shape
        out_specs = [pl.BlockSpec((None, c, GROUP_W), col(d, 0)) for d in range(2)] + out_specs
    return in_specs, args, out_shape, out_specs


def _hgrn_scan(p_b, p_f, lb_f, lb_b, s0_f, s0_b, with_output):
    in_specs, args, out_shape, out_specs = _hgrn_operands(p_b, p_f, lb_f, lb_b, s0_f, s0_b, with_output)
    res = pl.pallas_call(
        functools.partial(_hgrn_kernel, with_output=with_output),
        out_shape=out_shape,
        grid=(p_b.shape[0], p_b.shape[1] // HG_CHUNK),
        in_specs=in_specs,
        out_specs=out_specs,
        compiler_params=_params("parallel", "arbitrary"),
        name="hgrn",
    )(*args)
    return tuple(res) if with_output else (None, None, res[0], res[1])


def _mixer_kernel(*refs, n_attn_in, n_sub):
    attn_in, hgrn_in = refs[:n_attn_in], refs[n_attn_in:n_attn_in + 10]
    outs, vext_refs = refs[n_attn_in + 10:n_attn_in + 15], refs[n_attn_in + 15:]
    ins, o_refs, s_refs = _hgrn_refs(hgrn_in + outs[1:], True)
    _hgrn_load_state(ins, s_refs, 1)
    _fill_vext(attn_in[3::2], vext_refs)
    for k in range(MIXER_CHUNKS):
        back = MIXER_CHUNKS - 1 - k
        rows = (slice(k * HG_CHUNK, (k + 1) * HG_CHUNK), slice(back * HG_CHUNK, (back + 1) * HG_CHUNK))
        _emit_interleaved(
            _attn_stages(attn_in[0], attn_in[1], attn_in[2::2], vext_refs, outs[0], n_sub, tile=k,
                         n_tiles=MIXER_CHUNKS),
            _hgrn_stages(ins, o_refs, s_refs, rows))


def _latent_mixer(lam, pb_l, pf_l, pb_c, lb_f, lb_b, s0_f, s0_b):
    bsz, t, _ = pb_l.shape
    nc = t // (HG_CHUNK * MIXER_CHUNKS)
    tq = t * N_HEADS // nc
    nq = t // tq
    head_col = lambda grp: (lambda bi, ci: (bi, 0, grp * N_HEADS + ci // nq))
    in_specs = [
        pl.BlockSpec(memory_space=pltpu.SMEM),
        pl.BlockSpec((None, tq, HEAD_W), lambda bi, ci: (bi, ci % nq, PB_Q * N_HEADS + ci // nq)),
    ]
    args = [lam, pb_l]
    for src in (pb_l, pb_c):
        in_specs += [pl.BlockSpec((None, src.shape[1], HEAD_W), head_col(PB_K)),
                     pl.BlockSpec((None, src.shape[1], HEAD_W), head_col(PB_V))]
        args += [src, src]
    n_attn_in = len(args)
    h_specs, h_args, h_shape, h_out_specs = _hgrn_operands(pb_l, pf_l, lb_f, lb_b, s0_f, s0_b, True, MIXER_CHUNKS)
    res = pl.pallas_call(
        functools.partial(_mixer_kernel, n_attn_in=n_attn_in,
                          n_sub=max(tq // (MIXER_CHUNKS * ATTN_SUB_ROWS), 1)),
        out_shape=[jax.ShapeDtypeStruct((bsz, t, GROUP_W), F32)] + h_shape,
        grid=(bsz, nc),
        in_specs=in_specs + h_specs,
        out_specs=[pl.BlockSpec((None, tq, HEAD_W), lambda bi, ci: (bi, ci % nq, ci // nq))] + h_out_specs,
        scratch_shapes=[pltpu.VMEM((src.shape[1], 2 * HEAD_W), BF16) for src in (pb_l, pb_c)],
        compiler_params=_params("parallel", "arbitrary"),
        name="latent_mixer",
    )(*args, *h_args)
    return res[0], res[1], res[2]


def _merge_kernel(x_ref, mod_ref, a_ref, of_ref, ob_ref, gt_ref, dn_ref, hn_ref, w_ref, g_ref, b_ref, o_ref,
                  *, da_scale):
    a = a_ref[...]
    o = of_ref[...] + ob_ref[...]
    parts_a, parts_o = [], []
    for h in range(N_HEADS):
        sl = slice(h * HEAD_W, (h + 1) * HEAD_W)
        parts_a.append(_rms_norm(a[:, sl], dn_ref[:, sl]) * da_scale)
        parts_o.append(_rms_norm(o[:, sl], hn_ref[:, sl]))
    ya = jnp.concatenate(parts_a, axis=1)
    yo = jnp.concatenate(parts_o, axis=1) * _silu(gt_ref[...].astype(F32))
    y = jnp.dot(jnp.concatenate([ya, yo], axis=1).astype(BF16), w_ref[...], preferred_element_type=F32)
    z = ALPHA * x_ref[...] + mod_ref[5:6, :] * y
    o_ref[...] = _layer_norm(z, g_ref[1:2, :], b_ref[1:2, :])


def _merge(x, mods, layer, mod_row, attn, o_f, o_b, p_b, da_norm, hg_norm, w_out, j, ln_g, ln_b, da_scale):
    bsz, t, d = x.shape
    tm = _row_tile(t)
    grp = pl.BlockSpec((None, tm, GROUP_W), lambda bi, i: (bi, i, 0))
    vec = lambda n: pl.BlockSpec((None, 1, n), lambda bi, i: (j, 0, 0))
    ln = pl.BlockSpec((None, 3, d), lambda bi, i: (layer, 0, 0))
    return pl.pallas_call(
        functools.partial(_merge_kernel, da_scale=da_scale),
        out_shape=jax.ShapeDtypeStruct(x.shape, F32),
        grid=(bsz, t // tm),
        in_specs=[
            pl.BlockSpec((None, tm, d), lambda bi, i: (bi, i, 0)),
            _mod_spec(layer, mod_row, 2),
            grp, grp, grp,
            pl.BlockSpec((None, tm, GROUP_W), lambda bi, i: (bi, i, PB_HG)),
            vec(GROUP_W), vec(GROUP_W),
            pl.BlockSpec((None, d, d), lambda bi, i: (j, 0, 0)),
            ln, ln,
        ],
        out_specs=pl.BlockSpec((None, tm, d), lambda bi, i: (bi, i, 0)),
        compiler_params=_params("parallel", "parallel"),
        name="even_merge",
    )(x, mods, attn, o_f, o_b, p_b, da_norm, hg_norm, w_out, ln_g, ln_b)


def _glu_kernel(x_ref, mod_ref, w_ref, b_ref, o_ref):
    h = (x_ref[...] * (1.0 + mod_ref[4:5, :]) + mod_ref[3:4, :]).astype(BF16)
    y = jnp.dot(h, w_ref[...], preferred_element_type=F32) + b_ref[...]
    d = o_ref.shape[-1]
    o_ref[...] = y[:, :d] * jax.nn.sigmoid(y[:, d:])


def _glu(x, mods, layer, mod_row, w_pw1, b_pw1, j):
    bsz, t, d = x.shape
    tm = _row_tile(t)
    return pl.pallas_call(
        _glu_kernel,
        out_shape=jax.ShapeDtypeStruct(x.shape, F32),
        grid=(bsz, t // tm),
        in_specs=[
            pl.BlockSpec((None, tm, d), lambda bi, i: (bi, i, 0)),
            _mod_spec(layer, mod_row, 2),
            pl.BlockSpec((None, d, 2 * d), lambda bi, i: (j, 0, 0)),
            pl.BlockSpec((None, 1, 2 * d), lambda bi, i: (j, 0, 0)),
        ],
        out_specs=pl.BlockSpec((None, tm, d), lambda bi, i: (bi, i, 0)),
        compiler_params=_params("parallel", "parallel"),
        name="conv_glu",
    )(x, mods, w_pw1, b_pw1)


def _conv_kernel(x_ref, mod_ref, u_ref, up_ref, un_ref, wdw_ref, bdw_ref, cg_ref, cb_ref, w_ref, b2_ref,
                 g_ref, b_ref, o_ref, ext_ref, sh_ref, acc_ref):
    i = pl.program_id(1)
    tm = u_ref.shape[0]
    ext_ref[0:CONV_HALO, :] = jnp.where(i > 0, up_ref[...], 0.0)
    ext_ref[CONV_HALO:CONV_HALO + tm, :] = u_ref[...]
    ext_ref[CONV_HALO + tm:, :] = jnp.where(i < pl.num_programs(1) - 1, un_ref[...], 0.0)
    off = CONV_HALO - CONV_WIDTH // 2
    span = sh_ref.shape[1]
    for cb in range(u_ref.shape[1] // HEAD_W):
        cols = slice(cb * HEAD_W, (cb + 1) * HEAD_W)
        for p in range(1, 8):
            sh_ref[p - 1] = ext_ref[p:p + span, cols]

        def row_block(rb, carry):
            r0 = pl.multiple_of(rb * CONV_ROWS, CONV_ROWS)
            partial = [None] * CONV_PARTIALS
            for w in range(CONV_WIDTH):
                p, a = (off + w) % 8, (off + w) // 8
                rows = pl.ds(r0 + 8 * a, CONV_ROWS)
                win = ext_ref[rows, cols] if p == 0 else sh_ref[p - 1, rows, :]
                term = win * wdw_ref[w:w + 1, cols]
                k = w % CONV_PARTIALS
                partial[k] = term if partial[k] is None else partial[k] + term
            while len(partial) > 1:
                partial = [a + b for a, b in zip(partial[0::2], partial[1::2])]
            acc_ref[pl.ds(r0, CONV_ROWS), cols] = partial[0]
            return carry

        lax.fori_loop(0, tm // CONV_ROWS, row_block, 0)
    u = _silu(_layer_norm(acc_ref[...] + bdw_ref[...], cg_ref[...], cb_ref[...]))
    y = jnp.dot(u.astype(BF16), w_ref[...], preferred_element_type=F32) + b2_ref[...]
    z = ALPHA * x_ref[...] + mod_ref[5:6, :] * y
    o_ref[...] = _layer_norm(z, g_ref[1:2, :], b_ref[1:2, :])


def _conv(x, mods, layer, mod_row, u, w_dw, b_dw, cg, cb, w_pw2, b_pw2, j, ln_g, ln_b):
    bsz, t, d = x.shape
    tm = _row_tile(t)
    r = tm // CONV_HALO
    n_halo = t // CONV_HALO
    row = pl.BlockSpec((None, tm, d), lambda bi, i: (bi, i, 0))
    vec = pl.BlockSpec((None, 1, d), lambda bi, i: (j, 0, 0))
    ln = pl.BlockSpec((None, 3, d), lambda bi, i: (layer, 0, 0))
    return pl.pallas_call(
        _conv_kernel,
        out_shape=jax.ShapeDtypeStruct(x.shape, F32),
        grid=(bsz, t // tm),
        in_specs=[
            row,
            _mod_spec(layer, mod_row, 2),
            row,
            pl.BlockSpec((None, CONV_HALO, d), lambda bi, i: (bi, jnp.maximum(i * r - 1, 0), 0)),
            pl.BlockSpec((None, CONV_HALO, d), lambda bi, i: (bi, jnp.minimum((i + 1) * r, n_halo - 1), 0)),
            pl.BlockSpec((None, CONV_WIDTH, d), lambda bi, i: (j, 0, 0)),
            vec, vec, vec,
            pl.BlockSpec((None, d, d), lambda bi, i: (j, 0, 0)),
            vec, ln, ln,
        ],
        out_specs=row,
        scratch_shapes=[pltpu.VMEM((tm + 2 * CONV_HALO, d), F32),
                        pltpu.VMEM((7, tm + 8 * ((CONV_WIDTH + 7) // 8 - 1), HEAD_W), F32),
                        pltpu.VMEM((tm, d), F32)],
        compiler_params=_params("parallel", "parallel"),
        name="conv_mix",
    )(x, mods, u, u, u, w_dw, b_dw, cg, cb, w_pw2, b_pw2, ln_g, ln_b)


def kernel(x, c, ctx, c_ctx, w_mod, b_mod, ln_g, ln_b, ffn_w13, ffn_w2, ev_w_in, ev_w_out, da_lambda, da_norm,
           hg_lb, hg_norm, cv_w_pw1, cv_b_pw1, cv_w_dw, cv_b_dw, cv_ln_g, cv_ln_b, cv_w_pw2, cv_b_pw2):
    bsz, seq, d = x.shape
    per_layer_row = lambda a: a.reshape(a.shape[0], 1, a.shape[1])

    cc = jnp.zeros((16, d), F32).at[:bsz].set(c).at[bsz].set(c_ctx)
    mods = _mods(cc, w_mod, b_mod).reshape(DEPTH, 16, N_MOD, d)
    lat_row = lambda bi: bi
    ctx_row = lambda bi: bsz
    rope_tabs = _rope_tables(seq)
    lb_w = jax.nn.softmax(hg_lb.astype(F32), axis=0)
    lower_bounds = jnp.cumsum(lb_w, axis=0) - lb_w[:1]

    w13, w2 = ffn_w13.astype(BF16), ffn_w2.astype(BF16)
    w_in, w_out = ev_w_in.astype(BF16), ev_w_out.astype(BF16)
    w_pw1, w_pw2 = cv_w_pw1.astype(BF16), cv_w_pw2.astype(BF16)
    da_norm3, hg_norm3 = per_layer_row(da_norm), per_layer_row(hg_norm)
    conv_vecs = [per_layer_row(a) for a in (cv_b_dw, cv_ln_g, cv_ln_b)]
    b_pw1, b_pw2 = per_layer_row(cv_b_pw1), per_layer_row(cv_b_pw2)

    xl, xc = x, ctx
    for layer in range(DEPTH):
        j = layer // 2
        even = layer % 2 == 0
        ctx_pre = layer <= LAST_CTX_READ
        ctx_full = layer < LAST_CTX_READ
        ffn = lambda xx, mrow, fi, sub: _ffn(xx, mods, layer, mrow, w13, w2, fi, ln_g, ln_b, sub)

        xl = ffn(xl, lat_row, 0, 0)
        if ctx_pre:
            xc = ffn(xc, ctx_row, 0, 0)

        if even:
            lam_init = 0.8 - 0.6 * math.exp(-0.3 * layer)
            lv = da_lambda[j].astype(F32)
            lam = (jnp.exp(jnp.sum(lv[0] * lv[1])) - jnp.exp(jnp.sum(lv[2] * lv[3])) + lam_init).reshape(1)
            lb_f = lower_bounds[j, :GROUP_W].reshape(N_HEADS, HEAD_W)
            lb_b = lower_bounds[j, GROUP_W:].reshape(N_HEADS, HEAD_W)
            pb_l, pf_l = _inproj(xl, mods, layer, lat_row, w_in, j, rope_tabs)
            pb_c, pf_c = _inproj(xc, mods, layer, ctx_row, w_in, j, None)
            zeros = jnp.zeros((bsz, N_HEADS, HEAD_W, HEAD_W), F32)
            ocf, ocb, s_f, s_b = _hgrn_scan(pb_c, pf_c, lb_f, lb_b, zeros, zeros, ctx_full)
            a_l, olf, olb = _latent_mixer(lam, pb_l, pf_l, pb_c, lb_f, lb_b, s_f, s_b)
            merge = lambda xx, mrow, a, o_f, o_b, pb: _merge(
                xx, mods, layer, mrow, a, o_f, o_b, pb, da_norm3, hg_norm3, w_out, j, ln_g, ln_b, 1.0 - lam_init)
            if ctx_full:
                a_c = _attention(lam, pb_c, [pb_c])
                xc = merge(xc, ctx_row, a_c, ocf, ocb, pb_c)
            xl = merge(xl, lat_row, a_l, olf, olb, pb_l)
        else:
            conv = lambda xx, mrow, u: _conv(xx, mods, layer, mrow, u, cv_w_dw, *conv_vecs, w_pw2, b_pw2, j,
                                             ln_g, ln_b)
            if ctx_full:
                xc = conv(xc, ctx_row, _glu(xc, mods, layer, ctx_row, w_pw1, b_pw1, j))
            xl = conv(xl, lat_row, _glu(xl, mods, layer, lat_row, w_pw1, b_pw1, j))

        xl = ffn(xl, lat_row, 1, 2)
        if ctx_full:
            xc = ffn(xc, ctx_row, 1, 2)
    return xl
```
